```python
import jax, jax.numpy as jnp
from jax import lax
import numpy as np

D_MODEL = 4096
BATCH = 2
SEQ = 4096
DEPTH = 2

N_MIXERS = 2
GMLP_CHUNK = 128
GMLP_GROUPS = 8
D_MIX_A = D_MODEL
GMLP_GROUP_DIM = D_MIX_A // GMLP_GROUPS
HEAD_DIM = 128
N_HEADS = D_MODEL // HEAD_DIM
MOBA_BLOCK = 256
MOBA_TOPK = 3
Q_CHUNK = 32
ROPE_THETA = 10000.0
N_GROUPS = 4
EXPERTS_PER_GROUP = 8
N_EXPERTS = N_GROUPS * EXPERTS_PER_GROUP
TOPK_EXPERTS = 2
D_EXPERT = D_MODEL // 4
MOE_BLOCK = 256
NORM_EPS = 1e-6
NEG_INF = -1e30

kernel_name = 'hybrid_gmlp_moba_hmoe'


def rms_norm(x, g):
    xf = x.astype(jnp.float32)
    y = xf * lax.rsqrt(jnp.mean(xf * xf, axis=-1, keepdims=True) + NORM_EPS)
    return (y * g.astype(jnp.float32)).astype(x.dtype)


def layer_norm(x, g, b):
    xf = x.astype(jnp.float32)
    mu = jnp.mean(xf, axis=-1, keepdims=True)
    var = jnp.mean(jnp.square(xf - mu), axis=-1, keepdims=True)
    y = (xf - mu) * lax.rsqrt(var + NORM_EPS)
    return (y * g.astype(jnp.float32) + b.astype(jnp.float32)).astype(x.dtype)


def rope_tables(positions):
    inv_freq = ROPE_THETA ** (-jnp.arange(0, HEAD_DIM, 2, dtype=jnp.float32) / HEAD_DIM)
    ang = positions.astype(jnp.float32)[..., None] * inv_freq
    ang = jnp.concatenate([ang, ang], axis=-1)
    return jnp.cos(ang), jnp.sin(ang)


def apply_rope(x, cos, sin):
    xf = x.astype(jnp.float32)
    x1, x2 = jnp.split(xf, 2, axis=-1)
    rot = jnp.concatenate([-x2, x1], axis=-1)
    return (xf * cos[:, :, None, :] + rot * sin[:, :, None, :]).astype(x.dtype)


def chunked_gmlp(xn, w_in, ln_g, ln_b, w_s, b_s, w_out):
    B, S, _ = xn.shape
    z = jax.nn.gelu(xn @ w_in, approximate=False)
    u, v = jnp.split(z, 2, axis=-1)
    v = layer_norm(v, ln_g, ln_b)
    v = v.reshape(B, S // GMLP_CHUNK, GMLP_CHUNK, GMLP_GROUPS, GMLP_GROUP_DIM)
    causal = jnp.tril(jnp.ones((GMLP_CHUNK, GMLP_CHUNK), dtype=bool))
    w_causal = jnp.where(causal, w_s, 0).astype(v.dtype)
    s = jnp.einsum('gts,bcsgd->bctgd', w_causal, v) + b_s.T[:, :, None].astype(v.dtype)
    return (u * s.reshape(B, S, D_MIX_A)) @ w_out


def moba_attention(xn, positions, w_qkv, q_norm_g, k_norm_g, w_o):
    B, S, _ = xn.shape
    qkv = (xn @ w_qkv).reshape(B, S, 3, N_HEADS, HEAD_DIM)
    q = rms_norm(qkv[:, :, 0], q_norm_g)
    k = rms_norm(qkv[:, :, 1], k_norm_g)
    v = qkv[:, :, 2]
    cos, sin = rope_tables(positions)
    q = apply_rope(q, cos, sin).transpose(0, 2, 1, 3)
    k = apply_rope(k, cos, sin).transpose(0, 2, 1, 3)
    v = v.transpose(0, 2, 1, 3)
    n_blocks = -(-S // MOBA_BLOCK)
    pad = n_blocks * MOBA_BLOCK - S
    k = jnp.pad(k, ((0, 0), (0, 0), (0, pad), (0, 0)))
    v = jnp.pad(v, ((0, 0), (0, 0), (0, pad), (0, 0)))
    kb = k.reshape(B, N_HEADS, n_blocks, MOBA_BLOCK, HEAD_DIM)
    vb = v.reshape(B, N_HEADS, n_blocks, MOBA_BLOCK, HEAD_DIM)
    k_mean = jnp.mean(kb.astype(jnp.float32), axis=3)
    top_k = min(MOBA_TOPK, n_blocks)
    scale = HEAD_DIM ** -0.5
    b_idx = jnp.arange(B)[:, None, None, None]
    h_idx = jnp.arange(N_HEADS)[None, :, None, None]
    blk_ids = jnp.arange(n_blocks)
    slot_ids = jnp.arange(top_k)
    key_off = jnp.arange(MOBA_BLOCK)
    q_off = jnp.arange(Q_CHUNK)

    def attend_chunk(t0):
        j = t0 // MOBA_BLOCK
        qc = lax.dynamic_slice_in_dim(q, t0, Q_CHUNK, axis=2)
        gate = jnp.einsum('bhqd,bhnd->bhqn', qc.astype(jnp.float32), k_mean)
        gate = jnp.where(blk_ids < j, gate, NEG_INF)
        _, sel = lax.top_k(gate, top_k)
        valid = slot_ids < j
        k_sel = kb[b_idx, h_idx, sel]
        v_sel = vb[b_idx, h_idx, sel]
        s_sel = jnp.einsum('bhqd,bhqkpd->bhqkp', qc, k_sel,
                           preferred_element_type=jnp.float32) * scale
        s_sel = jnp.where(valid[:, None], s_sel, NEG_INF).reshape(B, N_HEADS, Q_CHUNK, top_k * MOBA_BLOCK)
        k_own = lax.dynamic_index_in_dim(kb, j, axis=2, keepdims=False)
        v_own = lax.dynamic_index_in_dim(vb, j, axis=2, keepdims=False)
        s_own = jnp.einsum('bhqd,bhpd->bhqp', qc, k_own,
                           preferred_element_type=jnp.float32) * scale
        causal = (j * MOBA_BLOCK + key_off)[None, :] <= (t0 + q_off)[:, None]
        s_own = jnp.where(causal, s_own, NEG_INF)
        p = jax.nn.softmax(jnp.concatenate([s_sel, s_own], axis=-1), axis=-1).astype(v.dtype)
        p_sel = p[..., :top_k * MOBA_BLOCK].reshape(B, N_HEADS, Q_CHUNK, top_k, MOBA_BLOCK)
        p_own = p[..., top_k * MOBA_BLOCK:]
        return (jnp.einsum('bhqkp,bhqkpd->bhqd', p_sel, v_sel)
                + jnp.einsum('bhqp,bhpd->bhqd', p_own, v_own))

    starts = jnp.arange(S // Q_CHUNK, dtype=jnp.int32) * Q_CHUNK
    o = lax.map(attend_chunk, starts)
    o = o.transpose(1, 0, 3, 2, 4).reshape(B, S, N_HEADS * HEAD_DIM)
    return o @ w_o


def hierarchical_moe(xn, router_group, router_expert, w1, w3, w2):
    B, S, D = xn.shape
    n_tok = B * S
    xt = xn.reshape(n_tok, D)
    group_prob = jax.nn.softmax((xt @ router_group).astype(jnp.float32), axis=-1)
    group_gate, group_idx = lax.top_k(group_prob, 1)
    expert_logits = (xt @ router_expert).astype(jnp.float32).reshape(n_tok, N_GROUPS, EXPERTS_PER_GROUP)
    in_group = jnp.take_along_axis(expert_logits, group_idx[:, :, None], axis=1)[:, 0]
    top_logit, top_local = lax.top_k(in_group, TOPK_EXPERTS)
    combine = group_gate * jax.nn.softmax(top_logit, axis=-1)
    expert_id = (group_idx * EXPERTS_PER_GROUP + top_local).reshape(-1)
    token_id = jnp.repeat(jnp.arange(n_tok, dtype=jnp.int32), TOPK_EXPERTS)
    weight = combine.reshape(-1)
    order = jnp.argsort(expert_id)
    e_sorted = expert_id[order]
    tok_sorted = token_id[order]
    w_sorted = weight[order]
    counts = jnp.bincount(expert_id, length=N_EXPERTS)
    padded = (counts + MOE_BLOCK - 1) // MOE_BLOCK * MOE_BLOCK
    pad_end = jnp.cumsum(padded)
    pad_start = pad_end - padded
    start = jnp.cumsum(counts) - counts
    n_assign = n_tok * TOPK_EXPERTS
    slot = pad_start[e_sorted] + jnp.arange(n_assign) - start[e_sorted]
    n_rows = n_assign + N_EXPERTS * MOE_BLOCK
    n_blk = n_rows // MOE_BLOCK
    buf = jnp.zeros((n_rows, D), xt.dtype).at[slot].set(xt[tok_sorted])
    blk_expert = jnp.minimum(
        jnp.searchsorted(pad_end, jnp.arange(n_blk, dtype=pad_end.dtype) * MOE_BLOCK, side='right'),
        N_EXPERTS - 1)

    def expert_block(args):
        xb, e = args
        h = jax.nn.silu(xb @ w1[e]) * (xb @ w3[e])
        return h @ w2[e]

    y = lax.map(expert_block, (buf.reshape(n_blk, MOE_BLOCK, D), blk_expert)).reshape(n_rows, D)
    contrib = y[slot].astype(jnp.float32) * w_sorted[:, None]
    out = jax.ops.segment_sum(contrib, tok_sorted, num_segments=n_tok)
    return out.astype(xn.dtype).reshape(B, S, D)


def setup_inputs(seed: int = 0) -> dict:
    key = jax.random.key(seed)
    ks = iter(jax.random.split(key, 32))
    f32 = jnp.float32

    def normal(shape, scale):
        return jax.random.normal(next(ks), shape, f32) * scale

    def gain(n):
        return 1.0 + normal((n,), 0.02)

    inp = {}
    inp['x'] = normal((BATCH, SEQ, D_MODEL), 1.0)
    inp['positions'] = jnp.broadcast_to(jnp.arange(SEQ, dtype=jnp.int32), (BATCH, SEQ))
    inp['l0_mixer_norm'] = gain(D_MODEL)
    inp['l0_gmlp_w_in'] = normal((D_MODEL, 2 * D_MIX_A), D_MODEL ** -0.5)
    inp['l0_gmlp_ln_g'] = gain(D_MIX_A)
    inp['l0_gmlp_ln_b'] = normal((D_MIX_A,), 0.02)
    inp['l0_gmlp_w_s'] = normal((GMLP_GROUPS, GMLP_CHUNK, GMLP_CHUNK), GMLP_CHUNK ** -0.5)
    inp['l0_gmlp_b_s'] = 1.0 + normal((GMLP_GROUPS, GMLP_CHUNK), 0.02)
    inp['l0_gmlp_w_out'] = normal((D_MIX_A, D_MODEL), D_MIX_A ** -0.5)
    inp['l0_ffn_norm'] = gain(D_MODEL)
    inp['l0_router_group'] = normal((D_MODEL, N_GROUPS), D_MODEL ** -0.5)
    inp['l0_router_expert'] = normal((D_MODEL, N_EXPERTS), D_MODEL ** -0.5)
    inp['l0_w1'] = normal((N_EXPERTS, D_MODEL, D_EXPERT), D_MODEL ** -0.5)
    inp['l0_w3'] = normal((N_EXPERTS, D_MODEL, D_EXPERT), D_MODEL ** -0.5)
    inp['l0_w2'] = normal((N_EXPERTS, D_EXPERT, D_MODEL), D_EXPERT ** -0.5)
    inp['l1_mixer_norm'] = gain(D_MODEL)
    inp['l1_w_qkv'] = normal((D_MODEL, 3 * N_HEADS * HEAD_DIM), D_MODEL ** -0.5)
    inp['l1_q_norm'] = gain(HEAD_DIM)
    inp['l1_k_norm'] = gain(HEAD_DIM)
    inp['l1_w_o'] = normal((N_HEADS * HEAD_DIM, D_MODEL), (N_HEADS * HEAD_DIM) ** -0.5)
    inp['l1_ffn_norm'] = gain(D_MODEL)
    inp['l1_router_group'] = normal((D_MODEL, N_GROUPS), D_MODEL ** -0.5)
    inp['l1_router_expert'] = normal((D_MODEL, N_EXPERTS), D_MODEL ** -0.5)
    inp['l1_w1'] = normal((N_EXPERTS, D_MODEL, D_EXPERT), D_MODEL ** -0.5)
    inp['l1_w3'] = normal((N_EXPERTS, D_MODEL, D_EXPERT), D_MODEL ** -0.5)
    inp['l1_w2'] = normal((N_EXPERTS, D_EXPERT, D_MODEL), D_EXPERT ** -0.5)
    return inp


def reference(x, positions,
              l0_mixer_norm, l0_gmlp_w_in, l0_gmlp_ln_g, l0_gmlp_ln_b, l0_gmlp_w_s, l0_gmlp_b_s,
              l0_gmlp_w_out, l0_ffn_norm, l0_router_group, l0_router_expert, l0_w1, l0_w3, l0_w2,
              l1_mixer_norm, l1_w_qkv, l1_q_norm, l1_k_norm, l1_w_o, l1_ffn_norm,
              l1_router_group, l1_router_expert, l1_w1, l1_w3, l1_w2):
    layers = [
        (l0_mixer_norm,
         (l0_gmlp_w_in, l0_gmlp_ln_g, l0_gmlp_ln_b, l0_gmlp_w_s, l0_gmlp_b_s, l0_gmlp_w_out),
         l0_ffn_norm, (l0_router_group, l0_router_expert, l0_w1, l0_w3, l0_w2)),
        (l1_mixer_norm,
         (l1_w_qkv, l1_q_norm, l1_k_norm, l1_w_o),
         l1_ffn_norm, (l1_router_group, l1_router_expert, l1_w1, l1_w3, l1_w2)),
    ]
    for i in range(DEPTH):
        mixer_norm, mixer_params, ffn_norm, ffn_params = layers[i]
        h = rms_norm(x, mixer_norm)
        if i % N_MIXERS == 0:
            x = x + chunked_gmlp(h, *mixer_params)
        else:
            x = x + moba_attention(h, positions, *mixer_params)
        x = x + hierarchical_moe(rms_norm(x, ffn_norm), *ffn_params)
    return x
```

```python
import functools

import jax
import jax.numpy as jnp
from jax import lax
from jax.experimental import pallas as pl
from jax.experimental.pallas import tpu as pltpu

NORM_EPS = 1e-6
NEG_INF = -1e30
ROPE_THETA = 10000.0
HEAD_DIM = 128
MOBA_BLOCK = 256
MOBA_TOPK = 3
TOPK_EXPERTS = 2
MOE_BLOCK = 256
LANES = 128
MIB = 1024 * 1024

BF16 = jnp.bfloat16
F32 = jnp.float32


def _params(vmem_mib, n_axes):
    return pltpu.CompilerParams(
        dimension_semantics=("arbitrary",) * n_axes,
        vmem_limit_bytes=vmem_mib * MIB)


def _rmsnorm_body(x_ref, g_ref, o_ref):
    x = x_ref[...]
    ms = jnp.mean(x * x, axis=-1, keepdims=True)
    o_ref[...] = (x * lax.rsqrt(ms + NORM_EPS) * g_ref[...]).astype(o_ref.dtype)


def _rmsnorm(x, g, tm=256):
    n, d = x.shape
    return pl.pallas_call(
        _rmsnorm_body,
        grid=(n // tm,),
        in_specs=[pl.BlockSpec((tm, d), lambda i: (i, 0)),
                  pl.BlockSpec((1, d), lambda i: (0, 0))],
        out_specs=pl.BlockSpec((tm, d), lambda i: (i, 0)),
        out_shape=jax.ShapeDtypeStruct((n, d), BF16),
        compiler_params=_params(32, 1),
        name="rmsnorm",
    )(x, g.reshape(1, d))


def _matmul_body(x_ref, w_ref, *rest, epilogue):
    if epilogue == "residual":
        r_ref, o_ref, wb_ref = rest
    else:
        o_ref, wb_ref = rest

    @pl.when(pl.program_id(1) == 0)
    def _():
        wb_ref[...] = w_ref[...].astype(BF16)

    acc = jnp.dot(x_ref[...], wb_ref[...], preferred_element_type=F32)
    if epilogue == "gelu":
        acc = 0.5 * acc * (1.0 + lax.erf(acc * (2.0 ** -0.5)))
    elif epilogue == "residual":
        acc = r_ref[...] + acc
    o_ref[...] = acc.astype(o_ref.dtype)


def _matmul(x, w, n_out, col_offset, epilogue, out_dtype, residual=None, tm=512, tn=512):
    m, k = x.shape
    col_block_offset = col_offset // tn
    in_specs = [pl.BlockSpec((tm, k), lambda j, i: (i, 0)),
                pl.BlockSpec((k, tn), lambda j, i: (0, j + col_block_offset))]
    args = [x, w]
    if epilogue == "residual":
        in_specs.append(pl.BlockSpec((tm, tn), lambda j, i: (i, j)))
        args.append(residual)
    return pl.pallas_call(
        functools.partial(_matmul_body, epilogue=epilogue),
        grid=(n_out // tn, m // tm),
        in_specs=in_specs,
        out_specs=pl.BlockSpec((tm, tn), lambda j, i: (i, j)),
        out_shape=jax.ShapeDtypeStruct((m, n_out), out_dtype),
        scratch_shapes=[pltpu.VMEM((k, tn), BF16)],
        compiler_params=_params(48, 2),
        name="matmul_" + epilogue,
    )(*args)


def _gate_body(u_ref, v_ref, lg_ref, lb_ref, ws_ref, bs_ref, o_ref, *, groups):
    v = v_ref[...]
    c, d = v.shape
    gd = d // groups
    mu = jnp.mean(v, axis=-1, keepdims=True)
    var = jnp.mean(jnp.square(v - mu), axis=-1, keepdims=True)
    vn = ((v - mu) * lax.rsqrt(var + NORM_EPS) * lg_ref[...] + lb_ref[...]).astype(BF16)
    row = lax.broadcasted_iota(jnp.int32, (c, c), 0)
    col = lax.broadcasted_iota(jnp.int32, (c, c), 1)
    causal = col <= row
    for g in range(groups):
        w = jnp.where(causal, ws_ref[g], 0.0).astype(BF16)
        s = jnp.dot(w, vn[:, g * gd:(g + 1) * gd], preferred_element_type=F32)
        s = s + bs_ref[:, g:g + 1]
        u = u_ref[:, g * gd:(g + 1) * gd].astype(F32)
        o_ref[:, g * gd:(g + 1) * gd] = (u * s).astype(o_ref.dtype)


def _spatial_gate(u, v, ln_g, ln_b, w_s, b_s):
    n, d = v.shape
    groups, c, _ = w_s.shape
    return pl.pallas_call(
        functools.partial(_gate_body, groups=groups),
        grid=(n // c,),
        in_specs=[pl.BlockSpec((c, d), lambda i: (i, 0)),
                  pl.BlockSpec((c, d), lambda i: (i, 0)),
                  pl.BlockSpec((1, d), lambda i: (0, 0)),
                  pl.BlockSpec((1, d), lambda i: (0, 0)),
                  pl.BlockSpec((groups, c, c), lambda i: (0, 0, 0)),
                  pl.BlockSpec((c, groups), lambda i: (0, 0))],
        out_specs=pl.BlockSpec((c, d), lambda i: (i, 0)),
        out_shape=jax.ShapeDtypeStruct((n, d), BF16),
        compiler_params=_params(32, 1),
        name="spatial_gate",
    )(u, v, ln_g.reshape(1, d), ln_b.reshape(1, d), w_s, b_s.T)


def _router_body(x_ref, g_ref, wr_ref, ids_ref, wts_ref, *, n_groups, per_group):
    x = x_ref[...]
    ms = jnp.mean(x * x, axis=-1, keepdims=True)
    xn = x * lax.rsqrt(ms + NORM_EPS) * g_ref[...]
    logits = jnp.dot(xn, wr_ref[...], preferred_element_type=F32,
                     precision=lax.Precision.HIGHEST)
    lane = lax.broadcasted_iota(jnp.int32, logits.shape, 1)
    n_exp = n_groups * per_group
    big = jnp.int32(LANES)
    neg = jnp.float32(-jnp.inf)

    def first_argmax(vals, vmax):
        return jnp.min(jnp.where(vals == vmax, lane, big), axis=-1, keepdims=True)

    gl = jnp.where(lane < n_groups, logits, neg)
    gmax = jnp.max(gl, axis=-1, keepdims=True)
    ge = jnp.exp(gl - gmax)
    gprob = ge / jnp.sum(ge, axis=-1, keepdims=True)
    gate = jnp.max(gprob, axis=-1, keepdims=True)
    gidx = first_argmax(gprob, gate)
    e_lane = lane - n_groups
    in_group = (e_lane >= 0) & (e_lane < n_exp) & ((e_lane // per_group) == gidx)
    el = jnp.where(in_group, logits, neg)
    m1 = jnp.max(el, axis=-1, keepdims=True)
    i1 = first_argmax(el, m1)
    el2 = jnp.where(lane == i1, neg, el)
    m2 = jnp.max(el2, axis=-1, keepdims=True)
    i2 = first_argmax(el2, m2)
    e2 = jnp.exp(m2 - m1)
    denom = 1.0 + e2
    w1 = gate * (1.0 / denom)
    w2 = gate * (e2 / denom)
    ids_ref[...] = jnp.where(lane == 0, i1 - n_groups, jnp.where(lane == 1, i2 - n_groups, 0))
    wts_ref[...] = jnp.where(lane == 0, w1, jnp.where(lane == 1, w2, 0.0))


def _router(x, g, router_group, router_expert, tm=256):
    n, d = x.shape
    n_groups = router_group.shape[1]
    n_exp = router_expert.shape[1]
    wr = jnp.concatenate(
        [router_group, router_expert, jnp.zeros((d, LANES - n_groups - n_exp), F32)], axis=1)
    ids, wts = pl.pallas_call(
        functools.partial(_router_body, n_groups=n_groups, per_group=n_exp // n_groups),
        grid=(n // tm,),
        in_specs=[pl.BlockSpec((tm, d), lambda i: (i, 0)),
                  pl.BlockSpec((1, d), lambda i: (0, 0)),
                  pl.BlockSpec((d, LANES), lambda i: (0, 0))],
        out_specs=[pl.BlockSpec((tm, LANES), lambda i: (i, 0)),
                   pl.BlockSpec((tm, LANES), lambda i: (i, 0))],
        out_shape=[jax.ShapeDtypeStruct((n, LANES), jnp.int32),
                   jax.ShapeDtypeStruct((n, LANES), F32)],
        compiler_params=_params(32, 1),
        name="router",
    )(x, g.reshape(1, d), wr)
    return ids[:, :TOPK_EXPERTS], wts[:, :TOPK_EXPERTS]


def _row_copy(src_hbm, dst_vmem, src_row, dst_row, sem):
    return pltpu.make_async_copy(src_hbm.at[pl.ds(src_row, 1)], dst_vmem.at[pl.ds(dst_row, 1)], sem)


def _dispatch_body(tok_ref, x_hbm, g_ref, o_ref, buf, sem):
    rows = buf.shape[0]

    def issue(r, carry):
        _row_copy(x_hbm, buf, tok_ref[0, 0, r], r, sem).start()
        return carry

    lax.fori_loop(0, rows, issue, 0)

    def drain(r, carry):
        _row_copy(x_hbm, buf, 0, r, sem).wait()
        return carry

    lax.fori_loop(0, rows, drain, 0)
    x = buf[...]
    ms = jnp.mean(x * x, axis=-1, keepdims=True)
    o_ref[...] = (x * lax.rsqrt(ms + NORM_EPS) * g_ref[...]).astype(o_ref.dtype)


def _dispatch(x, g, row_tok, n_blk):
    n, d = x.shape
    return pl.pallas_call(
        _dispatch_body,
        grid=(n_blk,),
        in_specs=[pl.BlockSpec((1, 1, MOE_BLOCK), lambda b: (b, 0, 0), memory_space=pltpu.SMEM),
                  pl.BlockSpec(memory_space=pl.ANY),
                  pl.BlockSpec((1, d), lambda b: (0, 0))],
        out_specs=pl.BlockSpec((MOE_BLOCK, d), lambda b: (b, 0)),
        out_shape=jax.ShapeDtypeStruct((n_blk * MOE_BLOCK, d), BF16),
        scratch_shapes=[pltpu.VMEM((MOE_BLOCK, d), F32), pltpu.SemaphoreType.DMA(())],
        compiler_params=_params(32, 1),
        name="moe_dispatch",
    )(row_tok.reshape(n_blk, 1, MOE_BLOCK), x, g.reshape(1, d))


def _up_body(se_ref, swj_ref, sj_ref, sb_ref, snew_ref, svalid_ref, xs_ref, w1_ref, w3_ref, h_ref,
             w1b_ref, w3b_ref):
    s = pl.program_id(0)

    @pl.when(snew_ref[s] == 1)
    def _():
        w1b_ref[...] = w1_ref[0].astype(BF16)
        w3b_ref[...] = w3_ref[0].astype(BF16)

    @pl.when(svalid_ref[s] == 1)
    def _():
        x = xs_ref[...]
        a = jnp.dot(x, w1b_ref[...], preferred_element_type=F32)
        c = jnp.dot(x, w3b_ref[...], preferred_element_type=F32)
        h_ref[...] = (jax.nn.silu(a) * c).astype(h_ref.dtype)

    @pl.when(svalid_ref[s] == 0)
    def _():
        h_ref[...] = jnp.zeros(h_ref.shape, h_ref.dtype)


def _expert_up(xs, w1, w3, sched, tf=512):
    n_rows, d = xs.shape
    f = w1.shape[2]
    n_steps = sched[0].shape[0]
    return pl.pallas_call(
        _up_body,
        grid_spec=pltpu.PrefetchScalarGridSpec(
            num_scalar_prefetch=6,
            grid=(n_steps,),
            in_specs=[pl.BlockSpec((MOE_BLOCK, d), lambda s, se, swj, sj, sb, sn, sv: (sb[s], 0)),
                      pl.BlockSpec((1, d, tf), lambda s, se, swj, sj, sb, sn, sv: (se[s], 0, swj[s])),
                      pl.BlockSpec((1, d, tf), lambda s, se, swj, sj, sb, sn, sv: (se[s], 0, swj[s]))],
            out_specs=pl.BlockSpec((MOE_BLOCK, tf), lambda s, se, swj, sj, sb, sn, sv: (sb[s], sj[s])),
            scratch_shapes=[pltpu.VMEM((d, tf), BF16), pltpu.VMEM((d, tf), BF16)]),
        out_shape=jax.ShapeDtypeStruct((n_rows, f), BF16),
        compiler_params=_params(56, 1),
        name="expert_up",
    )(*sched, xs, w1, w3)


def _down_body(be_ref, bnew_ref, bvalid_ref, h_ref, w2_ref, rw_ref, y_ref, w2b_ref):
    b = pl.program_id(0)

    @pl.when(bnew_ref[b] == 1)
    def _():
        w2b_ref[...] = w2_ref[0].astype(BF16)

    @pl.when(bvalid_ref[b] == 1)
    def _():
        y = jnp.dot(h_ref[...], w2b_ref[...], preferred_element_type=F32)
        y_ref[...] = y * rw_ref[...]

    @pl.when(bvalid_ref[b] == 0)
    def _():
        y_ref[...] = jnp.zeros(y_ref.shape, y_ref.dtype)


def _expert_down(h, w2, row_w, blk_e, blk_new, blk_valid):
    n_rows, f = h.shape
    d = w2.shape[2]
    n_blk = n_rows // MOE_BLOCK
    return pl.pallas_call(
        _down_body,
        grid_spec=pltpu.PrefetchScalarGridSpec(
            num_scalar_prefetch=3,
            grid=(n_blk,),
            in_specs=[pl.BlockSpec((MOE_BLOCK, f), lambda b, be, bn, bv: (b, 0)),
                      pl.BlockSpec((1, f, d), lambda b, be, bn, bv: (be[b], 0, 0)),
                      pl.BlockSpec((MOE_BLOCK, 1), lambda b, be, bn, bv: (b, 0))],
            out_specs=pl.BlockSpec((MOE_BLOCK, d), lambda b, be, bn, bv: (b, 0)),
            scratch_shapes=[pltpu.VMEM((f, d), BF16)]),
        out_shape=jax.ShapeDtypeStruct((n_rows, d), F32),
        compiler_params=_params(56, 1),
        name="expert_down",
    )(blk_e, blk_new, blk_valid, h, w2, row_w.reshape(n_rows, 1))


def _combine_body(slot_ref, x_ref, y_hbm, o_ref, buf, sem):
    tm = x_ref.shape[0]

    def issue(r, carry):
        for kk in range(TOPK_EXPERTS):
            _row_copy(y_hbm, buf.at[kk], slot_ref[0, kk, r], r, sem).start()
        return carry

    lax.fori_loop(0, tm, issue, 0)

    def drain(r, carry):
        for kk in range(TOPK_EXPERTS):
            _row_copy(y_hbm, buf.at[kk], 0, r, sem).wait()
        return carry

    lax.fori_loop(0, tm, drain, 0)
    acc = buf[0]
    for kk in range(1, TOPK_EXPERTS):
        acc = acc + buf[kk]
    o_ref[...] = x_ref[...] + acc


def _combine(x, y, slots, tm=256):
    n, d = x.shape
    slots_t = slots.reshape(n // tm, tm, TOPK_EXPERTS).transpose(0, 2, 1)
    return pl.pallas_call(
        _combine_body,
        grid=(n // tm,),
        in_specs=[pl.BlockSpec((1, TOPK_EXPERTS, tm), lambda i: (i, 0, 0), memory_space=pltpu.SMEM),
                  pl.BlockSpec((tm, d), lambda i: (i, 0)),
                  pl.BlockSpec(memory_space=pl.ANY)],
        out_specs=pl.BlockSpec((tm, d), lambda i: (i, 0)),
        out_shape=jax.ShapeDtypeStruct((n, d), F32),
        scratch_shapes=[pltpu.VMEM((TOPK_EXPERTS, tm, d), F32), pltpu.SemaphoreType.DMA(())],
        compiler_params=_params(40, 1),
        name="moe_combine",
    )(slots_t, x, y)


def _moe_plan(ids, wts, n_exp, tiles_per_expert):
    n_tok = ids.shape[0]
    n_assign = n_tok * TOPK_EXPERTS
    eid = ids.reshape(-1)
    onehot = (eid[:, None] == jnp.arange(n_exp, dtype=jnp.int32)[None, :]).astype(jnp.int32)
    csum = jnp.cumsum(onehot, axis=0)
    rank = jnp.take_along_axis(csum, eid[:, None], axis=1)[:, 0] - 1
    counts = csum[-1]
    padded = (counts + MOE_BLOCK - 1) // MOE_BLOCK * MOE_BLOCK
    pad_end = jnp.cumsum(padded)
    pad_start = pad_end - padded
    slot = pad_start[eid] + rank
    n_rows = n_assign + n_exp * MOE_BLOCK
    n_blk = n_rows // MOE_BLOCK
    token_id = jnp.arange(n_assign, dtype=jnp.int32) // TOPK_EXPERTS
    row_tok = jnp.zeros((n_rows,), jnp.int32).at[slot].set(token_id)
    row_w = jnp.zeros((n_rows,), F32).at[slot].set(wts.reshape(-1))
    blk_ids = jnp.arange(n_blk, dtype=jnp.int32)
    blk_e = jnp.minimum(
        jnp.searchsorted(pad_end, blk_ids * MOE_BLOCK, side='right'), n_exp - 1).astype(jnp.int32)
    n_used = (pad_end[-1] // MOE_BLOCK).astype(jnp.int32)
    last = jnp.maximum(n_used - 1, 0)
    blk_valid = (blk_ids < n_used).astype(jnp.int32)
    blk_e_eff = jnp.where(blk_ids < n_used, blk_e, blk_e[last])
    blk_new = jnp.concatenate(
        [jnp.ones((1,), jnp.int32), (blk_e_eff[1:] != blk_e_eff[:-1]).astype(jnp.int32)])
    blocks_of = (padded // MOE_BLOCK).astype(jnp.int32)
    first_blk = (pad_start // MOE_BLOCK).astype(jnp.int32)
    steps = jnp.arange(n_blk * tiles_per_expert, dtype=jnp.int32)
    n_valid = n_used * tiles_per_expert
    valid = steps < n_valid
    s_eff = jnp.where(valid, steps, jnp.maximum(n_valid - 1, 0))
    s_e = blk_e[s_eff // tiles_per_expert]
    local = s_eff - tiles_per_expert * first_blk[s_e]
    nb = jnp.maximum(blocks_of[s_e], 1)
    s_wj = (local // nb).astype(jnp.int32)
    extra = steps - n_valid
    s_j = jnp.where(valid, s_wj, extra % tiles_per_expert).astype(jnp.int32)
    s_b = jnp.where(valid, first_blk[s_e] + local % nb, n_used + extra // tiles_per_expert).astype(jnp.int32)
    key = s_e * tiles_per_expert + s_wj
    s_new = jnp.concatenate([jnp.ones((1,), jnp.int32), (key[1:] != key[:-1]).astype(jnp.int32)])
    sched = (s_e.astype(jnp.int32), s_wj, s_j, s_b, s_new, valid.astype(jnp.int32))
    return slot.reshape(n_tok, TOPK_EXPERTS), row_tok, row_w, n_blk, sched, (blk_e_eff, blk_new, blk_valid)


def _moe_layer(x, norm_g, router_group, router_expert, w1, w3, w2):
    n_exp, _, f = w1.shape
    tf = min(512, f)
    ids, wts = _router(x, norm_g, router_group, router_expert)
    slots, row_tok, row_w, n_blk, sched, down_sched = _moe_plan(ids, wts, n_exp, f // tf)
    xs = _dispatch(x, norm_g, row_tok, n_blk)
    h = _expert_up(xs, w1, w3, sched, tf=tf)
    y = _expert_down(h, w2, row_w, *down_sched)
    return _combine(x, y, slots)


def _rope_body(pos_ref, inv_ref, cos_ref, sin_ref):
    ang = pos_ref[...].astype(F32) * inv_ref[...]
    cos_ref[...] = jnp.cos(ang)
    sin_ref[...] = jnp.sin(ang)


def _rope_tables(positions, tm=1024):
    n = positions.size
    half = ROPE_THETA ** (-jnp.arange(0, HEAD_DIM, 2, dtype=F32) / HEAD_DIM)
    inv = jnp.concatenate([half, half]).reshape(1, HEAD_DIM)
    return pl.pallas_call(
        _rope_body,
        grid=(n // tm,),
        in_specs=[pl.BlockSpec((tm, 1), lambda i: (i, 0)),
                  pl.BlockSpec((1, HEAD_DIM), lambda i: (0, 0))],
        out_specs=[pl.BlockSpec((tm, HEAD_DIM), lambda i: (i, 0)),
                   pl.BlockSpec((tm, HEAD_DIM), lambda i: (i, 0))],
        out_shape=[jax.ShapeDtypeStruct((n, HEAD_DIM), F32),
                   jax.ShapeDtypeStruct((n, HEAD_DIM), F32)],
        compiler_params=_params(32, 1),
        name="rope_tables",
    )(positions.reshape(n, 1), inv)


def _qkv_body(x_ref, w_ref, g_ref, cos_ref, sin_ref, o_ref, wb_ref, *, tiles_per_part, heads_per_tile):
    j = pl.program_id(0)

    @pl.when(pl.program_id(1) == 0)
    def _():
        wb_ref[...] = w_ref[...].astype(BF16)

    acc = jnp.dot(x_ref[...], wb_ref[...], preferred_element_type=F32)

    @pl.when(j < 2 * tiles_per_part)
    def _():
        cos = cos_ref[...]
        sin = sin_ref[...]
        gain = g_ref[0]
        lane = lax.broadcasted_iota(jnp.int32, cos.shape, 1)
        for hh in range(heads_per_tile):
            a = acc[:, hh * HEAD_DIM:(hh + 1) * HEAD_DIM]
            ms = jnp.mean(a * a, axis=-1, keepdims=True)
            y = a * lax.rsqrt(ms + NORM_EPS) * gain
            rolled = pltpu.roll(y, HEAD_DIM // 2, axis=1)
            rot = jnp.where(lane < HEAD_DIM // 2, -rolled, rolled)
            o_ref[0, 0, hh] = (y * cos + rot * sin).astype(o_ref.dtype)

    @pl.when(j >= 2 * tiles_per_part)
    def _():
        for hh in range(heads_per_tile):
            o_ref[0, 0, hh] = acc[:, hh * HEAD_DIM:(hh + 1) * HEAD_DIM].astype(o_ref.dtype)


def _qkv_proj(x, w_qkv, q_gain, k_gain, cos, sin, batch, seq, tm=512, tn=512):
    m, k = x.shape
    d_att = w_qkv.shape[1] // 3
    n_heads = d_att // HEAD_DIM
    tiles_per_part = d_att // tn
    heads_per_tile = tn // HEAD_DIM
    m_per_batch = seq // tm
    gains = jnp.stack([q_gain, k_gain, jnp.ones_like(q_gain)]).reshape(3, 1, HEAD_DIM)
    return pl.pallas_call(
        functools.partial(_qkv_body, tiles_per_part=tiles_per_part, heads_per_tile=heads_per_tile),
        grid=(3 * tiles_per_part, m // tm),
        in_specs=[pl.BlockSpec((tm, k), lambda j, i: (i, 0)),
                  pl.BlockSpec((k, tn), lambda j, i: (0, j)),
                  pl.BlockSpec((1, 1, HEAD_DIM), lambda j, i: (j // tiles_per_part, 0, 0)),
                  pl.BlockSpec((tm, HEAD_DIM), lambda j, i: (i, 0)),
                  pl.BlockSpec((tm, HEAD_DIM), lambda j, i: (i, 0))],
        out_specs=pl.BlockSpec(
            (1, 1, heads_per_tile, tm, HEAD_DIM),
            lambda j, i: (j // tiles_per_part, i // m_per_batch, j % tiles_per_part,
                          i % m_per_batch, 0)),
        out_shape=jax.ShapeDtypeStruct((3, batch, n_heads, seq, HEAD_DIM), BF16),
        scratch_shapes=[pltpu.VMEM((k, tn), BF16)],
        compiler_params=_params(48, 2),
        name="qkv_proj",
    )(x, w_qkv, gains, cos, sin)


def _moba_body(q_ref, k_ref, v_ref, o_ref):
    q = q_ref[0, 0, 0]
    k = k_ref[0, 0, 0]
    v = v_ref[0, 0, 0]
    seq = q.shape[0]
    blk = MOBA_BLOCK
    n_blocks = seq // blk
    scale = HEAD_DIM ** -0.5
    nt_dims = (((1,), (1,)), ((), ()))

    bi = lax.broadcasted_iota(jnp.int32, (n_blocks, seq), 0)
    ki = lax.broadcasted_iota(jnp.int32, (n_blocks, seq), 1)
    indicator = jnp.where(ki // blk == bi, 1.0, 0.0).astype(BF16)
    k_mean = jnp.dot(indicator, k, preferred_element_type=F32) * (1.0 / blk)
    gate_all = lax.dot_general(q, k_mean.astype(BF16), nt_dims, preferred_element_type=F32)

    row = lax.broadcasted_iota(jnp.int32, (blk, blk), 0)
    col = lax.broadcasted_iota(jnp.int32, (blk, blk), 1)
    causal = col <= row
    blane = lax.broadcasted_iota(jnp.int32, (blk, n_blocks), 1)

    for j in range(n_blocks):
        qj = q[j * blk:(j + 1) * blk]
        s_own = lax.dot_general(qj, k[j * blk:(j + 1) * blk], nt_dims,
                                preferred_element_type=F32) * scale
        s_own = jnp.where(causal, s_own, NEG_INF)
        m = jnp.max(s_own, axis=-1, keepdims=True)
        n_past = j * blk
        if j > 0:
            g = jnp.where(blane < j, gate_all[j * blk:(j + 1) * blk], NEG_INF)
            sel = jnp.zeros((blk, n_blocks), F32)
            for _ in range(min(MOBA_TOPK, j)):
                gm = jnp.max(g, axis=-1, keepdims=True)
                pick = blane == jnp.min(jnp.where(g == gm, blane, n_blocks), axis=-1, keepdims=True)
                sel = jnp.where(pick, 1.0, sel)
                g = jnp.where(pick, -jnp.inf, g)
            sel_keys = jnp.dot(sel.astype(BF16), indicator[:, :n_past], preferred_element_type=F32)
            s_past = lax.dot_general(qj, k[:n_past], nt_dims, preferred_element_type=F32) * scale
            s_past = jnp.where(sel_keys > 0.5, s_past, NEG_INF)
            m = jnp.maximum(m, jnp.max(s_past, axis=-1, keepdims=True))
            e_past = jnp.exp(s_past - m)
        e_own = jnp.exp(s_own - m)
        total = jnp.sum(e_own, axis=-1, keepdims=True)
        if j > 0:
            total = total + jnp.sum(e_past, axis=-1, keepdims=True)
        inv = 1.0 / total
        o = jnp.dot((e_own * inv).astype(BF16), v[j * blk:(j + 1) * blk], preferred_element_type=F32)
        if j > 0:
            o = o + jnp.dot((e_past * inv).astype(BF16), v[:n_past], preferred_element_type=F32)
        o_ref[0, j * blk:(j + 1) * blk, :] = o.astype(o_ref.dtype)


def _moba_attention(qkv):
    _, batch, n_heads, seq, hd = qkv.shape

    def part(p):
        return pl.BlockSpec((1, 1, 1, seq, hd), lambda b, h: (p, b, h, 0, 0))

    return pl.pallas_call(
        _moba_body,
        grid=(batch, n_heads),
        in_specs=[part(0), part(1), part(2)],
        out_specs=pl.BlockSpec((1, seq, hd), lambda b, h: (b, 0, h)),
        out_shape=jax.ShapeDtypeStruct((batch, seq, n_heads * hd), BF16),
        compiler_params=_params(48, 2),
        name="moba_attention",
    )(qkv, qkv, qkv)


def kernel(x, positions, l0_mixer_norm, l0_gmlp_w_in, l0_gmlp_ln_g, l0_gmlp_ln_b, l0_gmlp_w_s, l0_gmlp_b_s, l0_gmlp_w_out, l0_ffn_norm, l0_router_group, l0_router_expert, l0_w1, l0_w3, l0_w2, l1_mixer_norm, l1_w_qkv, l1_q_norm, l1_k_norm, l1_w_o, l1_ffn_norm, l1_router_group, l1_router_expert, l1_w1, l1_w3, l1_w2):
    batch, seq, d = x.shape
    xt = x.reshape(batch * seq, d)

    h = _rmsnorm(xt, l0_mixer_norm)
    d_mix = l0_gmlp_w_in.shape[1] // 2
    u = _matmul(h, l0_gmlp_w_in, d_mix, 0, "gelu", BF16)
    v = _matmul(h, l0_gmlp_w_in, d_mix, d_mix, "gelu", F32)
    gated = _spatial_gate(u, v, l0_gmlp_ln_g, l0_gmlp_ln_b, l0_gmlp_w_s, l0_gmlp_b_s)
    xt = _matmul(gated, l0_gmlp_w_out, d, 0, "residual", F32, residual=xt)
    xt = _moe_layer(xt, l0_ffn_norm, l0_router_group, l0_router_expert, l0_w1, l0_w3, l0_w2)

    h = _rmsnorm(xt, l1_mixer_norm)
    cos, sin = _rope_tables(positions)
    qkv = _qkv_proj(h, l1_w_qkv, l1_q_norm, l1_k_norm, cos, sin, batch, seq)
    att = _moba_attention(qkv)
    xt = _matmul(att.reshape(batch * seq, d), l1_w_o, d, 0, "residual", F32, residual=xt)
    xt = _moe_layer(xt, l1_ffn_norm, l1_router_group, l1_router_expert, l1_w1, l1_w3, l1_w2)
    return xt.reshape(batch, seq, d)
```

```python
import functools

import jax
import jax.numpy as jnp
from jax import lax
from jax.experimental import pallas as pl
from jax.experimental.pallas import tpu as pltpu

NORM_EPS = 1e-6
NEG_INF = -1e30
ROPE_THETA = 10000.0
HEAD_DIM = 128
MOBA_BLOCK = 256
MOBA_TOPK = 3
TOPK_EXPERTS = 2
MOE_BLOCK = 256
SUB_ROWS = 256
LANES = 128
MIB = 1024 * 1024

BF16 = jnp.bfloat16
F32 = jnp.float32


def _params(vmem_mib, n_axes):
    return pltpu.CompilerParams(
        dimension_semantics=("arbitrary",) * n_axes,
        vmem_limit_bytes=vmem_mib * MIB)


def _rmsnorm_body(x_ref, g_ref, o_ref):
    x = x_ref[...]
    ms = jnp.mean(x * x, axis=-1, keepdims=True)
    o_ref[...] = (x * lax.rsqrt(ms + NORM_EPS) * g_ref[...]).astype(o_ref.dtype)


def _rmsnorm(x, g, tm=256):
    n, d = x.shape
    return pl.pallas_call(
        _rmsnorm_body,
        grid=(n // tm,),
        in_specs=[pl.BlockSpec((tm, d), lambda i: (i, 0)),
                  pl.BlockSpec((1, d), lambda i: (0, 0))],
        out_specs=pl.BlockSpec((tm, d), lambda i: (i, 0)),
        out_shape=jax.ShapeDtypeStruct((n, d), BF16),
        compiler_params=_params(32, 1),
        name="rmsnorm",
    )(x, g.reshape(1, d))


def _matmul_body(x_ref, w_ref, *rest, epilogue):
    if epilogue == "residual":
        r_ref, o_ref, wb_ref = rest
    else:
        o_ref, wb_ref = rest

    @pl.when(pl.program_id(1) == 0)
    def _():
        wb_ref[...] = w_ref[...].astype(BF16)

    sub = min(SUB_ROWS, x_ref.shape[0])
    for i in range(x_ref.shape[0] // sub):
        rows = pl.ds(i * sub, sub)
        acc = jnp.dot(x_ref[rows, :], wb_ref[...], preferred_element_type=F32)
        if epilogue == "gelu":
            acc = 0.5 * acc * (1.0 + lax.erf(acc * (2.0 ** -0.5)))
        elif epilogue == "residual":
            acc = r_ref[rows, :] + acc
        o_ref[rows, :] = acc.astype(o_ref.dtype)


def _matmul(x, w, n_out, col_offset, epilogue, out_dtype, residual=None, tm=1024, tn=512):
    m, k = x.shape
    tm = min(tm, m)
    col_block_offset = col_offset // tn
    in_specs = [pl.BlockSpec((tm, k), lambda j, i: (i, 0)),
                pl.BlockSpec((k, tn), lambda j, i: (0, j + col_block_offset))]
    args = [x, w]
    if epilogue == "residual":
        in_specs.append(pl.BlockSpec((tm, tn), lambda j, i: (i, j)))
        args.append(residual)
    return pl.pallas_call(
        functools.partial(_matmul_body, epilogue=epilogue),
        grid=(n_out // tn, m // tm),
        in_specs=in_specs,
        out_specs=pl.BlockSpec((tm, tn), lambda j, i: (i, j)),
        out_shape=jax.ShapeDtypeStruct((m, n_out), out_dtype),
        scratch_shapes=[pltpu.VMEM((k, tn), BF16)],
        compiler_params=_params(56, 2),
        name="matmul_" + epilogue,
    )(*args)


def _gate_body(u_ref, v_ref, lg_ref, lb_ref, ws_ref, bs_ref, o_ref, *, groups):
    v = v_ref[...]
    c, d = v.shape
    gd = d // groups
    mu = jnp.mean(v, axis=-1, keepdims=True)
    var = jnp.mean(jnp.square(v - mu), axis=-1, keepdims=True)
    vn = ((v - mu) * lax.rsqrt(var + NORM_EPS) * lg_ref[...] + lb_ref[...]).astype(BF16)
    row = lax.broadcasted_iota(jnp.int32, (c, c), 0)
    col = lax.broadcasted_iota(jnp.int32, (c, c), 1)
    causal = col <= row
    for g in range(groups):
        w = jnp.where(causal, ws_ref[g], 0.0).astype(BF16)
        s = jnp.dot(w, vn[:, g * gd:(g + 1) * gd], preferred_element_type=F32)
        s = s + bs_ref[:, g:g + 1]
        u = u_ref[:, g * gd:(g + 1) * gd].astype(F32)
        o_ref[:, g * gd:(g + 1) * gd] = (u * s).astype(o_ref.dtype)


def _spatial_gate(u, v, ln_g, ln_b, w_s, b_s):
    n, d = v.shape
    groups, c, _ = w_s.shape
    return pl.pallas_call(
        functools.partial(_gate_body, groups=groups),
        grid=(n // c,),
        in_specs=[pl.BlockSpec((c, d), lambda i: (i, 0)),
                  pl.BlockSpec((c, d), lambda i: (i, 0)),
                  pl.BlockSpec((1, d), lambda i: (0, 0)),
                  pl.BlockSpec((1, d), lambda i: (0, 0)),
                  pl.BlockSpec((groups, c, c), lambda i: (0, 0, 0)),
                  pl.BlockSpec((c, groups), lambda i: (0, 0))],
        out_specs=pl.BlockSpec((c, d), lambda i: (i, 0)),
        out_shape=jax.ShapeDtypeStruct((n, d), BF16),
        compiler_params=_params(32, 1),
        name="spatial_gate",
    )(u, v, ln_g.reshape(1, d), ln_b.reshape(1, d), w_s, b_s.T)


def _router_body(x_ref, g_ref, wr_ref, ids_ref, wts_ref, cnt_ref, run_ref, *, n_groups, per_group):
    @pl.when(pl.program_id(0) == 0)
    def _():
        run_ref[...] = jnp.zeros(run_ref.shape, run_ref.dtype)

    x = x_ref[...]
    tm = x.shape[0]
    ms = jnp.mean(x * x, axis=-1, keepdims=True)
    xn = x * lax.rsqrt(ms + NORM_EPS) * g_ref[...]
    logits = jnp.dot(xn.astype(BF16), wr_ref[...], preferred_element_type=F32)
    lane = lax.broadcasted_iota(jnp.int32, logits.shape, 1)
    n_exp = n_groups * per_group
    big = jnp.int32(LANES)
    neg = jnp.float32(-jnp.inf)

    def first_argmax(vals, vmax):
        return jnp.min(jnp.where(vals == vmax, lane, big), axis=-1, keepdims=True)

    gl = jnp.where(lane < n_groups, logits, neg)
    gmax = jnp.max(gl, axis=-1, keepdims=True)
    ge = jnp.exp(gl - gmax)
    gprob = ge / jnp.sum(ge, axis=-1, keepdims=True)
    gate = jnp.max(gprob, axis=-1, keepdims=True)
    gidx = first_argmax(gprob, gate)
    e_lane = lane - n_groups
    in_group = (e_lane >= 0) & (e_lane < n_exp) & ((e_lane // per_group) == gidx)
    el = jnp.where(in_group, logits, neg)
    m1 = jnp.max(el, axis=-1, keepdims=True)
    i1 = first_argmax(el, m1)
    el2 = jnp.where(lane == i1, neg, el)
    m2 = jnp.max(el2, axis=-1, keepdims=True)
    i2 = first_argmax(el2, m2)
    e2 = jnp.exp(m2 - m1)
    denom = 1.0 + e2
    w1 = gate * (1.0 / denom)
    w2 = gate * (e2 / denom)
    e1 = i1 - n_groups
    e2 = i2 - n_groups
    onehot = jnp.where(lane == e1, 1.0, 0.0) + jnp.where(lane == e2, 1.0, 0.0)
    r_i = lax.broadcasted_iota(jnp.int32, (tm, tm), 0)
    c_i = lax.broadcasted_iota(jnp.int32, (tm, tm), 1)
    earlier = jnp.where(c_i < r_i, 1.0, 0.0).astype(BF16)
    before = jnp.dot(earlier, onehot.astype(BF16), preferred_element_type=F32) + run_ref[...]
    rank1 = jnp.sum(jnp.where(lane == e1, before, 0.0), axis=-1, keepdims=True).astype(jnp.int32)
    rank2 = jnp.sum(jnp.where(lane == e2, before, 0.0), axis=-1, keepdims=True).astype(jnp.int32)
    run_ref[...] = run_ref[...] + jnp.sum(onehot, axis=0, keepdims=True)
    cnt_ref[...] = run_ref[...].astype(jnp.int32)
    ids_ref[...] = jnp.where(lane == 0, e1, jnp.where(lane == 1, e2, jnp.where(
        lane == 2, rank1, jnp.where(lane == 3, rank2, 0))))
    wts_ref[...] = jnp.where(lane == 0, w1, jnp.where(lane == 1, w2, 0.0))


def _router(x, g, router_group, router_expert, tm=256):
    n, d = x.shape
    n_groups = router_group.shape[1]
    n_exp = router_expert.shape[1]
    wr = jnp.concatenate(
        [router_group, router_expert, jnp.zeros((d, LANES - n_groups - n_exp), F32)], axis=1)
    return pl.pallas_call(
        functools.partial(_router_body, n_groups=n_groups, per_group=n_exp // n_groups),
        grid=(n // tm,),
        in_specs=[pl.BlockSpec((tm, d), lambda i: (i, 0)),
                  pl.BlockSpec((1, d), lambda i: (0, 0)),
                  pl.BlockSpec((d, LANES), lambda i: (0, 0))],
        out_specs=[pl.BlockSpec((tm, LANES), lambda i: (i, 0)),
                   pl.BlockSpec((tm, LANES), lambda i: (i, 0)),
                   pl.BlockSpec((1, LANES), lambda i: (0, 0))],
        out_shape=[jax.ShapeDtypeStruct((n, LANES), jnp.int32),
                   jax.ShapeDtypeStruct((n, LANES), F32),
                   jax.ShapeDtypeStruct((1, LANES), jnp.int32)],
        scratch_shapes=[pltpu.VMEM((1, LANES), F32)],
        compiler_params=_params(32, 1),
        name="router",
    )(x, g.reshape(1, d), wr.astype(BF16))


def _row_copy(src_hbm, dst_vmem, src_row, dst_row, sem):
    return pltpu.make_async_copy(src_hbm.at[pl.ds(src_row, 1)], dst_vmem.at[pl.ds(dst_row, 1)], sem)


def _dispatch_body(valid_ref, tok_ref, x_hbm, g_ref, o_ref, buf, sem):
    rows = buf.shape[0]

    @pl.when(valid_ref[pl.program_id(0)] == 1)
    def _():
        def issue(r, carry):
            _row_copy(x_hbm, buf, tok_ref[0, 0, r], r, sem).start()
            return carry

        lax.fori_loop(0, rows, issue, 0, unroll=8)

        def drain(r, carry):
            _row_copy(x_hbm, buf, 0, r, sem).wait()
            return carry

        lax.fori_loop(0, rows, drain, 0, unroll=8)
        x = buf[...]
        ms = jnp.mean(x * x, axis=-1, keepdims=True)
        o_ref[...] = (x * lax.rsqrt(ms + NORM_EPS) * g_ref[...]).astype(o_ref.dtype)

    @pl.when(valid_ref[pl.program_id(0)] == 0)
    def _():
        o_ref[...] = jnp.zeros(o_ref.shape, o_ref.dtype)


def _dispatch(x, g, row_tok, blk_valid):
    n, d = x.shape
    n_blk = blk_valid.shape[0]
    return pl.pallas_call(
        _dispatch_body,
        grid_spec=pltpu.PrefetchScalarGridSpec(
            num_scalar_prefetch=1,
            grid=(n_blk,),
            in_specs=[pl.BlockSpec((1, 1, MOE_BLOCK), lambda b, bv: (b, 0, 0), memory_space=pltpu.SMEM),
                      pl.BlockSpec(memory_space=pl.ANY),
                      pl.BlockSpec((1, d), lambda b, bv: (0, 0))],
            out_specs=pl.BlockSpec((MOE_BLOCK, d), lambda b, bv: (b, 0)),
            scratch_shapes=[pltpu.VMEM((MOE_BLOCK, d), F32), pltpu.SemaphoreType.DMA(())]),
        out_shape=jax.ShapeDtypeStruct((n_blk * MOE_BLOCK, d), BF16),
        compiler_params=_params(32, 1),
        name="moe_dispatch",
    )(blk_valid, row_tok.reshape(n_blk, 1, MOE_BLOCK), x, g.reshape(1, d))


def _up_body(se_ref, swj_ref, sj_ref, sb_ref, snew_ref, svalid_ref, xs_ref, w1_ref, w3_ref, h_ref,
             w1b_ref, w3b_ref):
    s = pl.program_id(0)

    @pl.when(snew_ref[s] == 1)
    def _():
        w1b_ref[...] = w1_ref[0].astype(BF16)
        w3b_ref[...] = w3_ref[0].astype(BF16)

    @pl.when(svalid_ref[s] == 1)
    def _():
        x = xs_ref[...]
        a = jnp.dot(x, w1b_ref[...], preferred_element_type=F32)
        c = jnp.dot(x, w3b_ref[...], preferred_element_type=F32)
        h_ref[...] = (jax.nn.silu(a) * c).astype(h_ref.dtype)

    @pl.when(svalid_ref[s] == 0)
    def _():
        h_ref[...] = jnp.zeros(h_ref.shape, h_ref.dtype)


def _expert_up(xs, w1, w3, sched, tf=512):
    n_rows, d = xs.shape
    f = w1.shape[2]
    n_steps = sched[0].shape[0]
    return pl.pallas_call(
        _up_body,
        grid_spec=pltpu.PrefetchScalarGridSpec(
            num_scalar_prefetch=6,
            grid=(n_steps,),
            in_specs=[pl.BlockSpec((MOE_BLOCK, d), lambda s, se, swj, sj, sb, sn, sv: (sb[s], 0)),
                      pl.BlockSpec((1, d, tf), lambda s, se, swj, sj, sb, sn, sv: (se[s], 0, swj[s])),
                      pl.BlockSpec((1, d, tf), lambda s, se, swj, sj, sb, sn, sv: (se[s], 0, swj[s]))],
            out_specs=pl.BlockSpec((MOE_BLOCK, tf), lambda s, se, swj, sj, sb, sn, sv: (sb[s], sj[s])),
            scratch_shapes=[pltpu.VMEM((d, tf), BF16), pltpu.VMEM((d, tf), BF16)]),
        out_shape=jax.ShapeDtypeStruct((n_rows, f), BF16),
        compiler_params=_params(56, 1),
        name="expert_up",
    )(*sched, xs, w1, w3)


def _down_body(be_ref, bnew_ref, bvalid_ref, h_ref, w2_ref, y_ref, w2b_ref):
    b = pl.program_id(0)

    @pl.when(bnew_ref[b] == 1)
    def _():
        w2b_ref[...] = w2_ref[0].astype(BF16)

    @pl.when(bvalid_ref[b] == 1)
    def _():
        y_ref[...] = jnp.dot(h_ref[...], w2b_ref[...], preferred_element_type=F32)

    @pl.when(bvalid_ref[b] == 0)
    def _():
        y_ref[...] = jnp.zeros(y_ref.shape, y_ref.dtype)


def _expert_down(h, w2, blk_e, blk_new, blk_valid):
    n_rows, f = h.shape
    d = w2.shape[2]
    n_blk = n_rows // MOE_BLOCK
    return pl.pallas_call(
        _down_body,
        grid_spec=pltpu.PrefetchScalarGridSpec(
            num_scalar_prefetch=3,
            grid=(n_blk,),
            in_specs=[pl.BlockSpec((MOE_BLOCK, f), lambda b, be, bn, bv: (b, 0)),
                      pl.BlockSpec((1, f, d), lambda b, be, bn, bv: (be[b], 0, 0))],
            out_specs=pl.BlockSpec((MOE_BLOCK, d), lambda b, be, bn, bv: (b, 0)),
            scratch_shapes=[pltpu.VMEM((f, d), BF16)]),
        out_shape=jax.ShapeDtypeStruct((n_rows, d), F32),
        compiler_params=_params(56, 1),
        name="expert_down",
    )(blk_e, blk_new, blk_valid, h, w2)


def _combine_body(slot_ref, x_ref, w_ref, y_hbm, o_ref, buf, sem):
    tm = x_ref.shape[0]

    def issue(r, carry):
        for kk in range(TOPK_EXPERTS):
            _row_copy(y_hbm, buf.at[kk], slot_ref[0, kk, r], r, sem).start()
        return carry

    lax.fori_loop(0, tm, issue, 0, unroll=4)

    def drain(r, carry):
        for kk in range(TOPK_EXPERTS):
            _row_copy(y_hbm, buf.at[kk], 0, r, sem).wait()
        return carry

    lax.fori_loop(0, tm, drain, 0, unroll=4)
    acc = buf[0] * w_ref[:, 0:1]
    for kk in range(1, TOPK_EXPERTS):
        acc = acc + buf[kk] * w_ref[:, kk:kk + 1]
    o_ref[...] = x_ref[...] + acc


def _combine(x, y, slots, wts, tm=256):
    n, d = x.shape
    slots_t = slots.reshape(n // tm, tm, TOPK_EXPERTS).transpose(0, 2, 1)
    return pl.pallas_call(
        _combine_body,
        grid=(n // tm,),
        in_specs=[pl.BlockSpec((1, TOPK_EXPERTS, tm), lambda i: (i, 0, 0), memory_space=pltpu.SMEM),
                  pl.BlockSpec((tm, d), lambda i: (i, 0)),
                  pl.BlockSpec((tm, LANES), lambda i: (i, 0)),
                  pl.BlockSpec(memory_space=pl.ANY)],
        out_specs=pl.BlockSpec((tm, d), lambda i: (i, 0)),
        out_shape=jax.ShapeDtypeStruct((n, d), F32),
        scratch_shapes=[pltpu.VMEM((TOPK_EXPERTS, tm, d), F32), pltpu.SemaphoreType.DMA(())],
        compiler_params=_params(40, 1),
        name="moe_combine",
    )(slots_t, x, wts, y)


def _moe_plan(ids, counts, n_exp, tiles_per_expert):
    n_tok = ids.shape[0]
    n_assign = n_tok * TOPK_EXPERTS
    eid = ids[:, :TOPK_EXPERTS].reshape(-1)
    rank = ids[:, TOPK_EXPERTS:2 * TOPK_EXPERTS].reshape(-1)
    counts = counts[0, :n_exp]
    padded = (counts + MOE_BLOCK - 1) // MOE_BLOCK * MOE_BLOCK
    pad_end = jnp.cumsum(padded)
    pad_start = pad_end - padded
    slot = pad_start[eid] + rank
    n_rows = n_assign + n_exp * MOE_BLOCK
    n_blk = n_rows // MOE_BLOCK
    token_id = jnp.arange(n_assign, dtype=jnp.int32) // TOPK_EXPERTS
    row_tok = (jnp.arange(n_rows, dtype=jnp.int32) % n_tok).at[slot].set(token_id)
    blk_ids = jnp.arange(n_blk, dtype=jnp.int32)
    blk_e = jnp.minimum(
        jnp.searchsorted(pad_end, blk_ids * MOE_BLOCK, side='right'), n_exp - 1).astype(jnp.int32)
    n_used = (pad_end[-1] // MOE_BLOCK).astype(jnp.int32)
    last = jnp.maximum(n_used - 1, 0)
    blk_valid = (blk_ids < n_used).astype(jnp.int32)
    blk_e_eff = jnp.where(blk_ids < n_used, blk_e, blk_e[last])
    blk_new = jnp.concatenate(
        [jnp.ones((1,), jnp.int32), (blk_e_eff[1:] != blk_e_eff[:-1]).astype(jnp.int32)])
    blocks_of = (padded // MOE_BLOCK).astype(jnp.int32)
    first_blk = (pad_start // MOE_BLOCK).astype(jnp.int32)
    steps = jnp.arange(n_blk * tiles_per_expert, dtype=jnp.int32)
    n_valid = n_used * tiles_per_expert
    valid = steps < n_valid
    s_eff = jnp.where(valid, steps, jnp.maximum(n_valid - 1, 0))
    s_e = blk_e[s_eff // tiles_per_expert]
    local = s_eff - tiles_per_expert * first_blk[s_e]
    nb = jnp.maximum(blocks_of[s_e], 1)
    s_wj = (local // nb).astype(jnp.int32)
    extra = steps - n_valid
    s_j = jnp.where(valid, s_wj, extra % tiles_per_expert).astype(jnp.int32)
    s_b = jnp.where(valid, first_blk[s_e] + local % nb, n_used + extra // tiles_per_expert).astype(jnp.int32)
    key = s_e * tiles_per_expert + s_wj
    s_new = jnp.concatenate([jnp.ones((1,), jnp.int32), (key[1:] != key[:-1]).astype(jnp.int32)])
    sched = (s_e.astype(jnp.int32), s_wj, s_j, s_b, s_new, valid.astype(jnp.int32))
    return slot.reshape(n_tok, TOPK_EXPERTS), row_tok, sched, (blk_e_eff, blk_new, blk_valid)


def _moe_layer(x, norm_g, router_group, router_expert, w1, w3, w2):
    n_exp, _, f = w1.shape
    tf = min(512, f)
    ids, wts, counts = _router(x, norm_g, router_group, router_expert)
    slots, row_tok, sched, down_sched = _moe_plan(ids, counts, n_exp, f // tf)
    xs = _dispatch(x, norm_g, row_tok, down_sched[2])
    h = _expert_up(xs, w1, w3, sched, tf=tf)
    y = _expert_down(h, w2, *down_sched)
    return _combine(x, y, slots, wts)


def _rope_body(pos_ref, inv_ref, cos_ref, sin_ref):
    ang = pos_ref[...].astype(F32) * inv_ref[...]
    cos_ref[...] = jnp.cos(ang)
    sin_ref[...] = jnp.sin(ang)


def _rope_tables(positions, tm=1024):
    n = positions.size
    half = ROPE_THETA ** (-jnp.arange(0, HEAD_DIM, 2, dtype=F32) / HEAD_DIM)
    inv = jnp.concatenate([half, half]).reshape(1, HEAD_DIM)
    return pl.pallas_call(
        _rope_body,
        grid=(n // tm,),
        in_specs=[pl.BlockSpec((tm, 1), lambda i: (i, 0)),
                  pl.BlockSpec((1, HEAD_DIM), lambda i: (0, 0))],
        out_specs=[pl.BlockSpec((tm, HEAD_DIM), lambda i: (i, 0)),
                   pl.BlockSpec((tm, HEAD_DIM), lambda i: (i, 0))],
        out_shape=[jax.ShapeDtypeStruct((n, HEAD_DIM), F32),
                   jax.ShapeDtypeStruct((n, HEAD_DIM), F32)],
        compiler_params=_params(32, 1),
        name="rope_tables",
    )(positions.reshape(n, 1), inv)


def _qkv_body(x_ref, w_ref, g_ref, cos_ref, sin_ref, o_ref, wb_ref, *, tiles_per_part, heads_per_tile):
    j = pl.program_id(0)

    @pl.when(pl.program_id(1) == 0)
    def _():
        wb_ref[...] = w_ref[...].astype(BF16)

    sub = min(SUB_ROWS, x_ref.shape[0])
    n_sub = x_ref.shape[0] // sub

    def sub_dot(i):
        return jnp.dot(x_ref[pl.ds(i * sub, sub), :], wb_ref[...], preferred_element_type=F32)

    @pl.when(j < 2 * tiles_per_part)
    def _():
        gain = g_ref[0]
        lane = lax.broadcasted_iota(jnp.int32, (sub, HEAD_DIM), 1)
        for i in range(n_sub):
            rows = pl.ds(i * sub, sub)
            acc = sub_dot(i)
            cos = cos_ref[rows, :]
            sin = sin_ref[rows, :]
            for hh in range(heads_per_tile):
                a = acc[:, hh * HEAD_DIM:(hh + 1) * HEAD_DIM]
                ms = jnp.mean(a * a, axis=-1, keepdims=True)
                y = a * lax.rsqrt(ms + NORM_EPS) * gain
                rolled = pltpu.roll(y, HEAD_DIM // 2, axis=1)
                rot = jnp.where(lane < HEAD_DIM // 2, -rolled, rolled)
                o_ref[0, 0, hh, rows, :] = (y * cos + rot * sin).astype(o_ref.dtype)

    @pl.when(j >= 2 * tiles_per_part)
    def _():
        for i in range(n_sub):
            acc = sub_dot(i)
            for hh in range(heads_per_tile):
                o_ref[0, 0, hh, pl.ds(i * sub, sub), :] = (
                    acc[:, hh * HEAD_DIM:(hh + 1) * HEAD_DIM].astype(o_ref.dtype))


def _qkv_proj(x, w_qkv, q_gain, k_gain, cos, sin, batch, seq, tm=1024, tn=512):
    m, k = x.shape
    tm = min(tm, seq)
    d_att = w_qkv.shape[1] // 3
    n_heads = d_att // HEAD_DIM
    tiles_per_part = d_att // tn
    heads_per_tile = tn // HEAD_DIM
    m_per_batch = seq // tm
    gains = jnp.stack([q_gain, k_gain, jnp.ones_like(q_gain)]).reshape(3, 1, HEAD_DIM)
    return pl.pallas_call(
        functools.partial(_qkv_body, tiles_per_part=tiles_per_part, heads_per_tile=heads_per_tile),
        grid=(3 * tiles_per_part, m // tm),
        in_specs=[pl.BlockSpec((tm, k), lambda j, i: (i, 0)),
                  pl.BlockSpec((k, tn), lambda j, i: (0, j)),
                  pl.BlockSpec((1, 1, HEAD_DIM), lambda j, i: (j // tiles_per_part, 0, 0)),
                  pl.BlockSpec((tm, HEAD_DIM), lambda j, i: (i, 0)),
                  pl.BlockSpec((tm, HEAD_DIM), lambda j, i: (i, 0))],
        out_specs=pl.BlockSpec(
            (1, 1, heads_per_tile, tm, HEAD_DIM),
            lambda j, i: (j // tiles_per_part, i // m_per_batch, j % tiles_per_part,
                          i % m_per_batch, 0)),
        out_shape=jax.ShapeDtypeStruct((3, batch, n_heads, seq, HEAD_DIM), BF16),
        scratch_shapes=[pltpu.VMEM((k, tn), BF16)],
        compiler_params=_params(56, 2),
        name="qkv_proj",
    )(x, w_qkv, gains, cos, sin)


def _moba_body(q_ref, k_ref, v_ref, o_ref):
    q = q_ref[0, 0, 0]
    k = k_ref[0, 0, 0]
    v = v_ref[0, 0, 0]
    seq = q.shape[0]
    blk = MOBA_BLOCK
    n_blocks = seq // blk
    exp2_scale = (HEAD_DIM ** -0.5) * 1.4426950408889634
    nt_dims = (((1,), (1,)), ((), ()))

    bi = lax.broadcasted_iota(jnp.int32, (LANES, seq), 0)
    ki = lax.broadcasted_iota(jnp.int32, (LANES, seq), 1)
    indicator = jnp.where(ki // blk == bi, 1.0, 0.0).astype(BF16)
    k_mean = jnp.dot(indicator, k, preferred_element_type=F32) * (1.0 / blk)
    gate_all = lax.dot_general(q, k_mean.astype(BF16), nt_dims, preferred_element_type=F32)

    kr = lax.broadcasted_iota(jnp.int32, (seq, LANES), 0)
    kl = lax.broadcasted_iota(jnp.int32, (seq, LANES), 1)
    k_ext = jnp.concatenate([k, jnp.where(kr // blk == kl, NEG_INF, 0.0).astype(BF16)], axis=1)

    row = lax.broadcasted_iota(jnp.int32, (blk, blk), 0)
    col = lax.broadcasted_iota(jnp.int32, (blk, blk), 1)
    causal = col <= row
    blane = lax.broadcasted_iota(jnp.int32, (blk, LANES), 1)

    for j in range(n_blocks):
        qj = q[j * blk:(j + 1) * blk]
        s_own = lax.dot_general(qj, k[j * blk:(j + 1) * blk], nt_dims, preferred_element_type=F32)
        s_own = jnp.where(causal, s_own, NEG_INF)
        m = jnp.max(s_own, axis=-1, keepdims=True)
        n_past = j * blk
        if j > 0:
            g = jnp.where(blane < j, gate_all[j * blk:(j + 1) * blk], NEG_INF)
            unselected = jnp.ones((blk, LANES), F32)
            for _ in range(min(MOBA_TOPK, j)):
                gm = jnp.max(g, axis=-1, keepdims=True)
                pick = blane == jnp.min(jnp.where(g == gm, blane, LANES), axis=-1, keepdims=True)
                unselected = jnp.where(pick, 0.0, unselected)
                g = jnp.where(pick, -jnp.inf, g)
            q_ext = jnp.concatenate([qj, unselected.astype(BF16)], axis=1)
            s_past = lax.dot_general(q_ext, k_ext[:n_past], nt_dims, preferred_element_type=F32)
            m = jnp.maximum(m, jnp.max(s_past, axis=-1, keepdims=True))
            e_past = jnp.exp2((s_past - m) * exp2_scale)
        e_own = jnp.exp2((s_own - m) * exp2_scale)
        total = jnp.sum(e_own, axis=-1, keepdims=True)
        o = jnp.dot(e_own.astype(BF16), v[j * blk:(j + 1) * blk], preferred_element_type=F32)
        if j > 0:
            total = total + jnp.sum(e_past, axis=-1, keepdims=True)
            o = o + jnp.dot(e_past.astype(BF16), v[:n_past], preferred_element_type=F32)
        o_ref[0, j * blk:(j + 1) * blk, :] = (o * (1.0 / total)).astype(o_ref.dtype)


def _moba_attention(qkv):
    _, batch, n_heads, seq, hd = qkv.shape

    def part(p):
        return pl.BlockSpec((1, 1, 1, seq, hd), lambda b, h: (p, b, h, 0, 0))

    return pl.pallas_call(
        _moba_body,
        grid=(batch, n_heads),
        in_specs=[part(0), part(1), part(2)],
        out_specs=pl.BlockSpec((1, seq, hd), lambda b, h: (b, 0, h)),
        out_shape=jax.ShapeDtypeStruct((batch, seq, n_heads * hd), BF16),
        compiler_params=_params(48, 2),
        name="moba_attention",
    )(qkv, qkv, qkv)


def kernel(x, positions, l0_mixer_norm, l0_gmlp_w_in, l0_gmlp_ln_g, l0_gmlp_ln_b, l0_gmlp_w_s, l0_gmlp_b_s, l0_gmlp_w_out, l0_ffn_norm, l0_router_group, l0_router_expert, l0_w1, l0_w3, l0_w2, l1_mixer_norm, l1_w_qkv, l1_q_norm, l1_k_norm, l1_w_o, l1_ffn_norm, l1_router_group, l1_router_expert, l1_w1, l1_w3, l1_w2):
    batch, seq, d = x.shape
    xt = x.reshape(batch * seq, d)

    h = _rmsnorm(xt, l0_mixer_norm)
    d_mix = l0_gmlp_w_in.shape[1] // 2
    u = _matmul(h, l0_gmlp_w_in, d_mix, 0, "gelu", BF16)
    v = _matmul(h, l0_gmlp_w_in, d_mix, d_mix, "gelu", F32)
    gated = _spatial_gate(u, v, l0_gmlp_ln_g, l0_gmlp_ln_b, l0_gmlp_w_s, l0_gmlp_b_s)
    xt = _matmul(gated, l0_gmlp_w_out, d, 0, "residual", F32, residual=xt)
    xt = _moe_layer(xt, l0_ffn_norm, l0_router_group, l0_router_expert, l0_w1, l0_w3, l0_w2)

    h = _rmsnorm(xt, l1_mixer_norm)
    cos, sin = _rope_tables(positions)
    qkv = _qkv_proj(h, l1_w_qkv, l1_q_norm, l1_k_norm, cos, sin, batch, seq)
    att = _moba_attention(qkv)
    xt = _matmul(att.reshape(batch * seq, d), l1_w_o, d, 0, "residual", F32, residual=xt)
    xt = _moe_layer(xt, l1_ffn_norm, l1_router_group, l1_router_expert, l1_w1, l1_w3, l1_w2)
    return xt.reshape(batch, seq, d)
```

```python
import functools

import jax
import jax.numpy as jnp
from jax import lax
from jax.experimental import pallas as pl
from jax.experimental.pallas import tpu as pltpu

NORM_EPS = 1e-6
NEG_INF = -1e30
ROPE_THETA = 10000.0
HEAD_DIM = 128
MOBA_BLOCK = 256
MOBA_TOPK = 3
TOPK_EXPERTS = 2
MOE_BLOCK = 256
SUB_ROWS = 256
LANES = 128
MIB = 1024 * 1024

BF16 = jnp.bfloat16
F32 = jnp.float32


def _params(vmem_mib, n_axes):
    return pltpu.CompilerParams(
        dimension_semantics=("arbitrary",) * n_axes,
        vmem_limit_bytes=vmem_mib * MIB)


def _rmsnorm_body(x_ref, g_ref, o_ref):
    x = x_ref[...]
    ms = jnp.mean(x * x, axis=-1, keepdims=True)
    o_ref[...] = (x * lax.rsqrt(ms + NORM_EPS) * g_ref[...]).astype(o_ref.dtype)


def _rmsnorm(x, g, tm=256):
    n, d = x.shape
    return pl.pallas_call(
        _rmsnorm_body,
        grid=(n // tm,),
        in_specs=[pl.BlockSpec((tm, d), lambda i: (i, 0)),
                  pl.BlockSpec((1, d), lambda i: (0, 0))],
        out_specs=pl.BlockSpec((tm, d), lambda i: (i, 0)),
        out_shape=jax.ShapeDtypeStruct((n, d), BF16),
        compiler_params=_params(32, 1),
        name="rmsnorm",
    )(x, g.reshape(1, d))


def _matmul_body(x_ref, w_ref, *rest, epilogue):
    if epilogue == "residual":
        r_ref, o_ref, wb_ref = rest
    else:
        o_ref, wb_ref = rest

    @pl.when(pl.program_id(1) == 0)
    def _():
        wb_ref[...] = w_ref[...].astype(BF16)

    sub = min(SUB_ROWS, x_ref.shape[0])
    for i in range(x_ref.shape[0] // sub):
        rows = pl.ds(i * sub, sub)
        acc = jnp.dot(x_ref[rows, :], wb_ref[...], preferred_element_type=F32)
        if epilogue == "gelu":
            acc = 0.5 * acc * (1.0 + lax.erf(acc * (2.0 ** -0.5)))
        elif epilogue == "residual":
            acc = r_ref[rows, :] + acc
        o_ref[rows, :] = acc.astype(o_ref.dtype)


def _matmul(x, w, n_out, col_offset, epilogue, out_dtype, residual=None, tm=1024, tn=512):
    m, k = x.shape
    tm = min(tm, m)
    col_block_offset = col_offset // tn
    in_specs = [pl.BlockSpec((tm, k), lambda j, i: (i, 0)),
                pl.BlockSpec((k, tn), lambda j, i: (0, j + col_block_offset))]
    args = [x, w]
    if epilogue == "residual":
        in_specs.append(pl.BlockSpec((tm, tn), lambda j, i: (i, j)))
        args.append(residual)
    return pl.pallas_call(
        functools.partial(_matmul_body, epilogue=epilogue),
        grid=(n_out // tn, m // tm),
        in_specs=in_specs,
        out_specs=pl.BlockSpec((tm, tn), lambda j, i: (i, j)),
        out_shape=jax.ShapeDtypeStruct((m, n_out), out_dtype),
        scratch_shapes=[pltpu.VMEM((k, tn), BF16)],
        compiler_params=_params(56, 2),
        name="matmul_" + epilogue,
    )(*args)


def _gate_body(u_ref, v_ref, lg_ref, lb_ref, ws_ref, bs_ref, o_ref, *, groups):
    v = v_ref[...]
    c, d = v.shape
    gd = d // groups
    mu = jnp.mean(v, axis=-1, keepdims=True)
    var = jnp.mean(jnp.square(v - mu), axis=-1, keepdims=True)
    vn = ((v - mu) * lax.rsqrt(var + NORM_EPS) * lg_ref[...] + lb_ref[...]).astype(BF16)
    row = lax.broadcasted_iota(jnp.int32, (c, c), 0)
    col = lax.broadcasted_iota(jnp.int32, (c, c), 1)
    causal = col <= row
    for g in range(groups):
        w = jnp.where(causal, ws_ref[g], 0.0).astype(BF16)
        s = jnp.dot(w, vn[:, g * gd:(g + 1) * gd], preferred_element_type=F32)
        s = s + bs_ref[:, g:g + 1]
        u = u_ref[:, g * gd:(g + 1) * gd].astype(F32)
        o_ref[:, g * gd:(g + 1) * gd] = (u * s).astype(o_ref.dtype)


def _spatial_gate(u, v, ln_g, ln_b, w_s, b_s):
    n, d = v.shape
    groups, c, _ = w_s.shape
    return pl.pallas_call(
        functools.partial(_gate_body, groups=groups),
        grid=(n // c,),
        in_specs=[pl.BlockSpec((c, d), lambda i: (i, 0)),
                  pl.BlockSpec((c, d), lambda i: (i, 0)),
                  pl.BlockSpec((1, d), lambda i: (0, 0)),
                  pl.BlockSpec((1, d), lambda i: (0, 0)),
                  pl.BlockSpec((groups, c, c), lambda i: (0, 0, 0)),
                  pl.BlockSpec((c, groups), lambda i: (0, 0))],
        out_specs=pl.BlockSpec((c, d), lambda i: (i, 0)),
        out_shape=jax.ShapeDtypeStruct((n, d), BF16),
        compiler_params=_params(32, 1),
        name="spatial_gate",
    )(u, v, ln_g.reshape(1, d), ln_b.reshape(1, d), w_s, b_s.T)


def _router_body(x_ref, g_ref, wr_ref, ids_ref, wts_ref, cnt_ref, xp_ref, run_ref, *, n_groups, per_group):
    @pl.when(pl.program_id(0) == 0)
    def _():
        run_ref[...] = jnp.zeros(run_ref.shape, run_ref.dtype)

    x = x_ref[...]
    tm = x.shape[0]
    ms = jnp.mean(x * x, axis=-1, keepdims=True)
    xn = (x * lax.rsqrt(ms + NORM_EPS) * g_ref[...]).astype(BF16)
    half = xn.shape[1] // 2
    lo = lax.bitcast_convert_type(xn[:, :half].astype(F32), jnp.uint32) >> 16
    hi = lax.bitcast_convert_type(xn[:, half:].astype(F32), jnp.uint32) & jnp.uint32(0xFFFF0000)
    xp_ref[...] = hi | lo
    logits = jnp.dot(xn, wr_ref[...], preferred_element_type=F32)
    lane = lax.broadcasted_iota(jnp.int32, logits.shape, 1)
    n_exp = n_groups * per_group
    big = jnp.int32(LANES)
    neg = jnp.float32(-jnp.inf)

    def first_argmax(vals, vmax):
        return jnp.min(jnp.where(vals == vmax, lane, big), axis=-1, keepdims=True)

    gl = jnp.where(lane < n_groups, logits, neg)
    gmax = jnp.max(gl, axis=-1, keepdims=True)
    ge = jnp.exp(gl - gmax)
    gprob = ge / jnp.sum(ge, axis=-1, keepdims=True)
    gate = jnp.max(gprob, axis=-1, keepdims=True)
    gidx = first_argmax(gprob, gate)
    e_lane = lane - n_groups
    in_group = (e_lane >= 0) & (e_lane < n_exp) & ((e_lane // per_group) == gidx)
    el = jnp.where(in_group, logits, neg)
    m1 = jnp.max(el, axis=-1, keepdims=True)
    i1 = first_argmax(el, m1)
    el2 = jnp.where(lane == i1, neg, el)
    m2 = jnp.max(el2, axis=-1, keepdims=True)
    i2 = first_argmax(el2, m2)
    e2 = jnp.exp(m2 - m1)
    denom = 1.0 + e2
    w1 = gate * (1.0 / denom)
    w2 = gate * (e2 / denom)
    e1 = i1 - n_groups
    e2 = i2 - n_groups
    onehot = jnp.where(lane == e1, 1.0, 0.0) + jnp.where(lane == e2, 1.0, 0.0)
    r_i = lax.broadcasted_iota(jnp.int32, (tm, tm), 0)
    c_i = lax.broadcasted_iota(jnp.int32, (tm, tm), 1)
    earlier = jnp.where(c_i < r_i, 1.0, 0.0).astype(BF16)
    before = jnp.dot(earlier, onehot.astype(BF16), preferred_element_type=F32) + run_ref[...]
    rank1 = jnp.sum(jnp.where(lane == e1, before, 0.0), axis=-1, keepdims=True).astype(jnp.int32)
    rank2 = jnp.sum(jnp.where(lane == e2, before, 0.0), axis=-1, keepdims=True).astype(jnp.int32)
    run_ref[...] = run_ref[...] + jnp.sum(onehot, axis=0, keepdims=True)
    cnt_ref[...] = run_ref[...].astype(jnp.int32)
    ids_ref[...] = jnp.where(lane == 0, e1, jnp.where(lane == 1, e2, jnp.where(
        lane == 2, rank1, jnp.where(lane == 3, rank2, 0))))
    wts_ref[...] = jnp.where(lane == 0, w1, jnp.where(lane == 1, w2, 0.0))


def _router(x, g, router_group, router_expert, tm=256):
    n, d = x.shape
    n_groups = router_group.shape[1]
    n_exp = router_expert.shape[1]
    wr = jnp.concatenate(
        [router_group, router_expert, jnp.zeros((d, LANES - n_groups - n_exp), F32)], axis=1)
    return pl.pallas_call(
        functools.partial(_router_body, n_groups=n_groups, per_group=n_exp // n_groups),
        grid=(n // tm,),
        in_specs=[pl.BlockSpec((tm, d), lambda i: (i, 0)),
                  pl.BlockSpec((1, d), lambda i: (0, 0)),
                  pl.BlockSpec((d, LANES), lambda i: (0, 0))],
        out_specs=[pl.BlockSpec((tm, LANES), lambda i: (i, 0)),
                   pl.BlockSpec((tm, LANES), lambda i: (i, 0)),
                   pl.BlockSpec((1, LANES), lambda i: (0, 0)),
                   pl.BlockSpec((tm, d // 2), lambda i: (i, 0))],
        out_shape=[jax.ShapeDtypeStruct((n, LANES), jnp.int32),
                   jax.ShapeDtypeStruct((n, LANES), F32),
                   jax.ShapeDtypeStruct((1, LANES), jnp.int32),
                   jax.ShapeDtypeStruct((n, d // 2), jnp.uint32)],
        scratch_shapes=[pltpu.VMEM((1, LANES), F32)],
        compiler_params=_params(32, 1),
        name="router",
    )(x, g.reshape(1, d), wr.astype(BF16))


def _row_copy(src_hbm, dst_vmem, src_row, dst_row, sem):
    return pltpu.make_async_copy(src_hbm.at[pl.ds(src_row, 1)], dst_vmem.at[pl.ds(dst_row, 1)], sem)


def _dispatch_body(valid_ref, tok_ref, xp_hbm, o_ref, buf, sem):
    rows, half = buf.shape

    @pl.when(valid_ref[pl.program_id(0)] == 1)
    def _():
        def issue(r, carry):
            _row_copy(xp_hbm, buf, tok_ref[0, 0, r], r, sem).start()
            return carry

        lax.fori_loop(0, rows, issue, 0, unroll=8)

        def drain(r, carry):
            _row_copy(xp_hbm, buf, 0, r, sem).wait()
            return carry

        lax.fori_loop(0, rows, drain, 0, unroll=8)
        w = buf[...]
        o_ref[:, :half] = lax.bitcast_convert_type(w << 16, F32).astype(o_ref.dtype)
        o_ref[:, half:] = lax.bitcast_convert_type(w & jnp.uint32(0xFFFF0000), F32).astype(o_ref.dtype)

    @pl.when(valid_ref[pl.program_id(0)] == 0)
    def _():
        o_ref[...] = jnp.zeros(o_ref.shape, o_ref.dtype)


def _dispatch(xp, row_tok, blk_valid):
    n, half = xp.shape
    n_blk = blk_valid.shape[0]
    return pl.pallas_call(
        _dispatch_body,
        grid_spec=pltpu.PrefetchScalarGridSpec(
            num_scalar_prefetch=1,
            grid=(n_blk,),
            in_specs=[pl.BlockSpec((1, 1, MOE_BLOCK), lambda b, bv: (b, 0, 0), memory_space=pltpu.SMEM),
                      pl.BlockSpec(memory_space=pl.ANY)],
            out_specs=pl.BlockSpec((MOE_BLOCK, 2 * half), lambda b, bv: (b, 0)),
            scratch_shapes=[pltpu.VMEM((MOE_BLOCK, half), jnp.uint32), pltpu.SemaphoreType.DMA(())]),
        out_shape=jax.ShapeDtypeStruct((n_blk * MOE_BLOCK, 2 * half), BF16),
        compiler_params=_params(32, 1),
        name="moe_dispatch",
    )(blk_valid, row_tok.reshape(n_blk, 1, MOE_BLOCK), xp)


def _up_body(se_ref, swj_ref, sj_ref, sb_ref, snew_ref, svalid_ref, sslot_ref, sne_ref, snj_ref,
             snext_ref, xs_ref, w1_hbm, w3_hbm, h_ref, stage, w1b_ref, w3b_ref, sem):
    s = pl.program_id(0)
    tf = w1b_ref.shape[1]
    slot = sslot_ref[s]

    def tile_copies(e, j, dst_slot):
        cols = pl.ds(pl.multiple_of(j * tf, tf), tf)
        return (pltpu.make_async_copy(w1_hbm.at[e, :, cols], stage.at[dst_slot, 0], sem.at[dst_slot, 0]),
                pltpu.make_async_copy(w3_hbm.at[e, :, cols], stage.at[dst_slot, 1], sem.at[dst_slot, 1]))

    def gated_unit(x, w1b, w3b):
        a = jnp.dot(x, w1b, preferred_element_type=F32)
        c = jnp.dot(x, w3b, preferred_element_type=F32)
        return (jax.nn.silu(a) * c).astype(h_ref.dtype)

    @pl.when(s == 0)
    def _():
        for cp in tile_copies(se_ref[0], swj_ref[0], slot):
            cp.start()

    @pl.when(snew_ref[s] == 1)
    def _():
        for cp in tile_copies(se_ref[s], swj_ref[s], slot):
            cp.wait()

        @pl.when(snext_ref[s] == 1)
        def _():
            for cp in tile_copies(sne_ref[s], snj_ref[s], 1 - slot):
                cp.start()

        w1b = stage[slot, 0].astype(BF16)
        w3b = stage[slot, 1].astype(BF16)
        w1b_ref[...] = w1b
        w3b_ref[...] = w3b
        h_ref[...] = gated_unit(xs_ref[...], w1b, w3b)

    @pl.when((snew_ref[s] == 0) & (svalid_ref[s] == 1))
    def _():
        h_ref[...] = gated_unit(xs_ref[...], w1b_ref[...], w3b_ref[...])

    @pl.when(svalid_ref[s] == 0)
    def _():
        h_ref[...] = jnp.zeros(h_ref.shape, h_ref.dtype)


def _expert_up(xs, w1, w3, sched, tf=512):
    n_rows, d = xs.shape
    f = w1.shape[2]
    n_steps = sched[0].shape[0]
    n_pref = len(sched)
    return pl.pallas_call(
        _up_body,
        grid_spec=pltpu.PrefetchScalarGridSpec(
            num_scalar_prefetch=n_pref,
            grid=(n_steps,),
            in_specs=[pl.BlockSpec((MOE_BLOCK, d), lambda s, *p: (p[3][s], 0)),
                      pl.BlockSpec(memory_space=pl.ANY),
                      pl.BlockSpec(memory_space=pl.ANY)],
            out_specs=pl.BlockSpec((MOE_BLOCK, tf), lambda s, *p: (p[3][s], p[2][s])),
            scratch_shapes=[pltpu.VMEM((2, 2, d, tf), F32),
                            pltpu.VMEM((d, tf), BF16), pltpu.VMEM((d, tf), BF16),
                            pltpu.SemaphoreType.DMA((2, 2))]),
        out_shape=jax.ShapeDtypeStruct((n_rows, f), BF16),
        compiler_params=_params(56, 1),
        name="expert_up",
    )(*sched, xs, w1, w3)


def _down_body(be_ref, bnew_ref, bvalid_ref, bslot_ref, bne_ref, bnext_ref, h_ref, w2_hbm, y_ref,
               stage, w2b_ref, sem):
    b = pl.program_id(0)
    slot = bslot_ref[b]

    def weight_copy(e, dst_slot):
        return pltpu.make_async_copy(w2_hbm.at[e], stage.at[dst_slot], sem.at[dst_slot])

    @pl.when(b == 0)
    def _():
        weight_copy(be_ref[0], slot).start()

    @pl.when(bnew_ref[b] == 1)
    def _():
        weight_copy(be_ref[b], slot).wait()

        @pl.when(bnext_ref[b] == 1)
        def _():
            weight_copy(bne_ref[b], 1 - slot).start()

        w2b = stage[slot].astype(BF16)
        w2b_ref[...] = w2b
        y_ref[...] = jnp.dot(h_ref[...], w2b, preferred_element_type=F32)

    @pl.when((bnew_ref[b] == 0) & (bvalid_ref[b] == 1))
    def _():
        y_ref[...] = jnp.dot(h_ref[...], w2b_ref[...], preferred_element_type=F32)

    @pl.when(bvalid_ref[b] == 0)
    def _():
        y_ref[...] = jnp.zeros(y_ref.shape, y_ref.dtype)


def _expert_down(h, w2, down_sched):
    n_rows, f = h.shape
    d = w2.shape[2]
    n_blk = n_rows // MOE_BLOCK
    return pl.pallas_call(
        _down_body,
        grid_spec=pltpu.PrefetchScalarGridSpec(
            num_scalar_prefetch=len(down_sched),
            grid=(n_blk,),
            in_specs=[pl.BlockSpec((MOE_BLOCK, f), lambda b, *p: (b, 0)),
                      pl.BlockSpec(memory_space=pl.ANY)],
            out_specs=pl.BlockSpec((MOE_BLOCK, d), lambda b, *p: (b, 0)),
            scratch_shapes=[pltpu.VMEM((2, f, d), F32), pltpu.VMEM((f, d), BF16),
                            pltpu.SemaphoreType.DMA((2,))]),
        out_shape=jax.ShapeDtypeStruct((n_rows, d), F32),
        compiler_params=_params(56, 1),
        name="expert_down",
    )(*down_sched, h, w2)


def _combine_body(slot_ref, x_ref, w_ref, y_hbm, o_ref, buf, sem):
    tm = x_ref.shape[0]

    def issue(r, carry):
        for kk in range(TOPK_EXPERTS):
            _row_copy(y_hbm, buf.at[kk], slot_ref[0, kk, r], r, sem).start()
        return carry

    lax.fori_loop(0, tm, issue, 0, unroll=4)

    def drain(r, carry):
        for kk in range(TOPK_EXPERTS):
            _row_copy(y_hbm, buf.at[kk], 0, r, sem).wait()
        return carry

    lax.fori_loop(0, tm, drain, 0, unroll=4)
    acc = buf[0] * w_ref[:, 0:1]
    for kk in range(1, TOPK_EXPERTS):
        acc = acc + buf[kk] * w_ref[:, kk:kk + 1]
    o_ref[...] = x_ref[...] + acc


def _combine(x, y, slots, wts, tm=256):
    n, d = x.shape
    slots_t = slots.reshape(n // tm, tm, TOPK_EXPERTS).transpose(0, 2, 1)
    return pl.pallas_call(
        _combine_body,
        grid=(n // tm,),
        in_specs=[pl.BlockSpec((1, TOPK_EXPERTS, tm), lambda i: (i, 0, 0), memory_space=pltpu.SMEM),
                  pl.BlockSpec((tm, d), lambda i: (i, 0)),
                  pl.BlockSpec((tm, LANES), lambda i: (i, 0)),
                  pl.BlockSpec(memory_space=pl.ANY)],
        out_specs=pl.BlockSpec((tm, d), lambda i: (i, 0)),
        out_shape=jax.ShapeDtypeStruct((n, d), F32),
        scratch_shapes=[pltpu.VMEM((TOPK_EXPERTS, tm, d), F32), pltpu.SemaphoreType.DMA(())],
        compiler_params=_params(40, 1),
        name="moe_combine",
    )(slots_t, x, wts, y)


def _moe_plan(ids, counts, n_exp, tiles_per_expert):
    n_tok = ids.shape[0]
    n_assign = n_tok * TOPK_EXPERTS
    eid = ids[:, :TOPK_EXPERTS].reshape(-1)
    rank = ids[:, TOPK_EXPERTS:2 * TOPK_EXPERTS].reshape(-1)
    counts = counts[0, :n_exp]
    padded = (counts + MOE_BLOCK - 1) // MOE_BLOCK * MOE_BLOCK
    pad_end = jnp.cumsum(padded)
    pad_start = pad_end - padded
    slot = pad_start[eid] + rank
    n_rows = n_assign + n_exp * MOE_BLOCK
    n_blk = n_rows // MOE_BLOCK
    token_id = jnp.arange(n_assign, dtype=jnp.int32) // TOPK_EXPERTS
    row_tok = (jnp.arange(n_rows, dtype=jnp.int32) % n_tok).at[slot].set(token_id)
    blk_ids = jnp.arange(n_blk, dtype=jnp.int32)
    blk_e = jnp.minimum(
        jnp.searchsorted(pad_end, blk_ids * MOE_BLOCK, side='right'), n_exp - 1).astype(jnp.int32)
    n_used = (pad_end[-1] // MOE_BLOCK).astype(jnp.int32)
    last = jnp.maximum(n_used - 1, 0)
    blk_valid = (blk_ids < n_used).astype(jnp.int32)
    blk_e_eff = jnp.where(blk_ids < n_used, blk_e, blk_e[last])
    blk_new = jnp.concatenate(
        [jnp.ones((1,), jnp.int32), (blk_e_eff[1:] != blk_e_eff[:-1]).astype(jnp.int32)])
    blk_slot, blk_next_at, blk_has_next = _group_lookahead(blk_new)
    down_sched = (blk_e_eff, blk_new, blk_valid, blk_slot, blk_e_eff[blk_next_at], blk_has_next)
    blocks_of = (padded // MOE_BLOCK).astype(jnp.int32)
    first_blk = (pad_start // MOE_BLOCK).astype(jnp.int32)
    steps = jnp.arange(n_blk * tiles_per_expert, dtype=jnp.int32)
    n_valid = n_used * tiles_per_expert
    valid = steps < n_valid
    s_eff = jnp.where(valid, steps, jnp.maximum(n_valid - 1, 0))
    s_e = blk_e[s_eff // tiles_per_expert]
    local = s_eff - tiles_per_expert * first_blk[s_e]
    nb = jnp.maximum(blocks_of[s_e], 1)
    s_wj = (local // nb).astype(jnp.int32)
    extra = steps - n_valid
    s_j = jnp.where(valid, s_wj, extra % tiles_per_expert).astype(jnp.int32)
    s_b = jnp.where(valid, first_blk[s_e] + local % nb, n_used + extra // tiles_per_expert).astype(jnp.int32)
    key = s_e * tiles_per_expert + s_wj
    s_new = jnp.concatenate([jnp.ones((1,), jnp.int32), (key[1:] != key[:-1]).astype(jnp.int32)])
    s_e = s_e.astype(jnp.int32)
    s_slot, s_next_at, s_has_next = _group_lookahead(s_new)
    sched = (s_e, s_wj, s_j, s_b, s_new, valid.astype(jnp.int32),
             s_slot, s_e[s_next_at], s_wj[s_next_at], s_has_next)
    return slot.reshape(n_tok, TOPK_EXPERTS), row_tok, sched, down_sched


def _group_lookahead(new_flags):
    n = new_flags.shape[0]
    idx = jnp.arange(n, dtype=jnp.int32)
    slot = ((jnp.cumsum(new_flags) - 1) % 2).astype(jnp.int32)
    starts_from = lax.cummin(jnp.where(new_flags == 1, idx, n), reverse=True)
    next_at = jnp.concatenate([starts_from[1:], jnp.full((1,), n, jnp.int32)])
    has_next = (next_at < n).astype(jnp.int32)
    return slot, jnp.minimum(next_at, n - 1), has_next


def _moe_layer(x, norm_g, router_group, router_expert, w1, w3, w2):
    n_exp, _, f = w1.shape
    tf = min(512, f)
    ids, wts, counts, xp = _router(x, norm_g, router_group, router_expert)
    slots, row_tok, sched, down_sched = _moe_plan(ids, counts, n_exp, f // tf)
    xs = _dispatch(xp, row_tok, down_sched[2])
    h = _expert_up(xs, w1, w3, sched, tf=tf)
    y = _expert_down(h, w2, down_sched)
    return _combine(x, y, slots, wts)


def _rope_body(pos_ref, inv_ref, cos_ref, sin_ref):
    ang = pos_ref[...].astype(F32) * inv_ref[...]
    cos_ref[...] = jnp.cos(ang)
    sin_ref[...] = jnp.sin(ang)


def _rope_tables(positions, tm=1024):
    n = positions.size
    half = ROPE_THETA ** (-jnp.arange(0, HEAD_DIM, 2, dtype=F32) / HEAD_DIM)
    inv = jnp.concatenate([half, half]).reshape(1, HEAD_DIM)
    return pl.pallas_call(
        _rope_body,
        grid=(n // tm,),
        in_specs=[pl.BlockSpec((tm, 1), lambda i: (i, 0)),
                  pl.BlockSpec((1, HEAD_DIM), lambda i: (0, 0))],
        out_specs=[pl.BlockSpec((tm, HEAD_DIM), lambda i: (i, 0)),
                   pl.BlockSpec((tm, HEAD_DIM), lambda i: (i, 0))],
        out_shape=[jax.ShapeDtypeStruct((n, HEAD_DIM), F32),
                   jax.ShapeDtypeStruct((n, HEAD_DIM), F32)],
        compiler_params=_params(32, 1),
        name="rope_tables",
    )(positions.reshape(n, 1), inv)


def _qkv_body(x_ref, w_ref, g_ref, cos_ref, sin_ref, o_ref, wb_ref, *, tiles_per_part, heads_per_tile):
    j = pl.program_id(0)

    @pl.when(pl.program_id(1) == 0)
    def _():
        wb_ref[...] = w_ref[...].astype(BF16)

    sub = min(SUB_ROWS, x_ref.shape[0])
    n_sub = x_ref.shape[0] // sub

    def sub_dot(i):
        return jnp.dot(x_ref[pl.ds(i * sub, sub), :], wb_ref[...], preferred_element_type=F32)

    @pl.when(j < 2 * tiles_per_part)
    def _():
        gain = g_ref[0]
        lane = lax.broadcasted_iota(jnp.int32, (sub, HEAD_DIM), 1)
        for i in range(n_sub):
            rows = pl.ds(i * sub, sub)
            acc = sub_dot(i)
            cos = cos_ref[rows, :]
            sin = sin_ref[rows, :]
            for hh in range(heads_per_tile):
                a = acc[:, hh * HEAD_DIM:(hh + 1) * HEAD_DIM]
                ms = jnp.mean(a * a, axis=-1, keepdims=True)
                y = a * lax.rsqrt(ms + NORM_EPS) * gain
                rolled = pltpu.roll(y, HEAD_DIM // 2, axis=1)
                rot = jnp.where(lane < HEAD_DIM // 2, -rolled, rolled)
                o_ref[0, 0, hh, rows, :] = (y * cos + rot * sin).astype(o_ref.dtype)

    @pl.when(j >= 2 * tiles_per_part)
    def _():
        for i in range(n_sub):
            acc = sub_dot(i)
            for hh in range(heads_per_tile):
                o_ref[0, 0, hh, pl.ds(i * sub, sub), :] = (
                    acc[:, hh * HEAD_DIM:(hh + 1) * HEAD_DIM].astype(o_ref.dtype))


def _qkv_proj(x, w_qkv, q_gain, k_gain, cos, sin, batch, seq, tm=1024, tn=512):
    m, k = x.shape
    tm = min(tm, seq)
    d_att = w_qkv.shape[1] // 3
    n_heads = d_att // HEAD_DIM
    tiles_per_part = d_att // tn
    heads_per_tile = tn // HEAD_DIM
    m_per_batch = seq // tm
    gains = jnp.stack([q_gain, k_gain, jnp.ones_like(q_gain)]).reshape(3, 1, HEAD_DIM)
    return pl.pallas_call(
        functools.partial(_qkv_body, tiles_per_part=tiles_per_part, heads_per_tile=heads_per_tile),
        grid=(3 * tiles_per_part, m // tm),
        in_specs=[pl.BlockSpec((tm, k), lambda j, i: (i, 0)),
                  pl.BlockSpec((k, tn), lambda j, i: (0, j)),
                  pl.BlockSpec((1, 1, HEAD_DIM), lambda j, i: (j // tiles_per_part, 0, 0)),
                  pl.BlockSpec((tm, HEAD_DIM), lambda j, i: (i, 0)),
                  pl.BlockSpec((tm, HEAD_DIM), lambda j, i: (i, 0))],
        out_specs=pl.BlockSpec(
            (1, 1, heads_per_tile, tm, HEAD_DIM),
            lambda j, i: (j // tiles_per_part, i // m_per_batch, j % tiles_per_part,
                          i % m_per_batch, 0)),
        out_shape=jax.ShapeDtypeStruct((3, batch, n_heads, seq, HEAD_DIM), BF16),
        scratch_shapes=[pltpu.VMEM((k, tn), BF16)],
        compiler_params=_params(56, 2),
        name="qkv_proj",
    )(x, w_qkv, gains, cos, sin)


def _moba_body(q_ref, k_ref, v_ref, o_ref):
    q = q_ref[0, 0, 0]
    k = k_ref[0, 0, 0]
    v = v_ref[0, 0, 0]
    seq = q.shape[0]
    blk = MOBA_BLOCK
    n_blocks = seq // blk
    exp2_scale = (HEAD_DIM ** -0.5) * 1.4426950408889634
    nt_dims = (((1,), (1,)), ((), ()))

    bi = lax.broadcasted_iota(jnp.int32, (LANES, seq), 0)
    ki = lax.broadcasted_iota(jnp.int32, (LANES, seq), 1)
    indicator = jnp.where(ki // blk == bi, 1.0, 0.0).astype(BF16)
    k_mean = jnp.dot(indicator, k, preferred_element_type=F32) * (1.0 / blk)
    gate_all = lax.dot_general(q, k_mean.astype(BF16), nt_dims, preferred_element_type=F32)

    kr = lax.broadcasted_iota(jnp.int32, (seq, LANES), 0)
    kl = lax.broadcasted_iota(jnp.int32, (seq, LANES), 1)
    k_ext = jnp.concatenate([k, jnp.where(kr // blk == kl, NEG_INF, 0.0).astype(BF16)], axis=1)

    row = lax.broadcasted_iota(jnp.int32, (blk, blk), 0)
    col = lax.broadcasted_iota(jnp.int32, (blk, blk), 1)
    causal = col <= row
    blane = lax.broadcasted_iota(jnp.int32, (blk, LANES), 1)

    for j in range(n_blocks):
        qj = q[j * blk:(j + 1) * blk]
        s_own = lax.dot_general(qj, k[j * blk:(j + 1) * blk], nt_dims, preferred_element_type=F32)
        s_own = jnp.where(causal, s_own, NEG_INF)
        m = jnp.max(s_own, axis=-1, keepdims=True)
        n_past = j * blk
        if j > 0:
            g = jnp.where(blane < j, gate_all[j * blk:(j + 1) * blk], NEG_INF)
            unselected = jnp.ones((blk, LANES), F32)
            for _ in range(min(MOBA_TOPK, j)):
                gm = jnp.max(g, axis=-1, keepdims=True)
                pick = blane == jnp.min(jnp.where(g == gm, blane, LANES), axis=-1, keepdims=True)
                unselected = jnp.where(pick, 0.0, unselected)
                g = jnp.where(pick, -jnp.inf, g)
            q_ext = jnp.concatenate([qj, unselected.astype(BF16)], axis=1)
            s_past = lax.dot_general(q_ext, k_ext[:n_past], nt_dims, preferred_element_type=F32)
            m = jnp.maximum(m, jnp.max(s_past, axis=-1, keepdims=True))
            e_past = jnp.exp2((s_past - m) * exp2_scale)
        e_own = jnp.exp2((s_own - m) * exp2_scale)
        total = jnp.sum(e_own, axis=-1, keepdims=True)
        o = jnp.dot(e_own.astype(BF16), v[j * blk:(j + 1) * blk], preferred_element_type=F32)
        if j > 0:
            total = total + jnp.sum(e_past, axis=-1, keepdims=True)
            o = o + jnp.dot(e_past.astype(BF16), v[:n_past], preferred_element_type=F32)
        o_ref[0, j * blk:(j + 1) * blk, :] = (o * (1.0 / total)).astype(o_ref.dtype)


def _moba_attention(qkv):
    _, batch, n_heads, seq, hd = qkv.shape

    def part(p):
        return pl.BlockSpec((1, 1, 1, seq, hd), lambda b, h: (p, b, h, 0, 0))

    return pl.pallas_call(
        _moba_body,
        grid=(batch, n_heads),
        in_specs=[part(0), part(1), part(2)],
        out_specs=pl.BlockSpec((1, seq, hd), lambda b, h: (b, 0, h)),
        out_shape=jax.ShapeDtypeStruct((batch, seq, n_heads * hd), BF16),
        compiler_params=_params(48, 2),
        name="moba_attention",
    )(qkv, qkv, qkv)


def kernel(x, positions, l0_mixer_norm, l0_gmlp_w_in, l0_gmlp_ln_g, l0_gmlp_ln_b, l0_gmlp_w_s, l0_gmlp_b_s, l0_gmlp_w_out, l0_ffn_norm, l0_router_group, l0_router_expert, l0_w1, l0_w3, l0_w2, l1_mixer_norm, l1_w_qkv, l1_q_norm, l1_k_norm, l1_w_o, l1_ffn_norm, l1_router_group, l1_router_expert, l1_w1, l1_w3, l1_w2):
    batch, seq, d = x.shape
    xt = x.reshape(batch * seq, d)

    h = _rmsnorm(xt, l0_mixer_norm)
    d_mix = l0_gmlp_w_in.shape[1] // 2
    u = _matmul(h, l0_gmlp_w_in, d_mix, 0, "gelu", BF16)
    v = _matmul(h, l0_gmlp_w_in, d_mix, d_mix, "gelu", F32)
    gated = _spatial_gate(u, v, l0_gmlp_ln_g, l0_gmlp_ln_b, l0_gmlp_w_s, l0_gmlp_b_s)
    xt = _matmul(gated, l0_gmlp_w_out, d, 0, "residual", F32, residual=xt)
    xt = _moe_layer(xt, l0_ffn_norm, l0_router_group, l0_router_expert, l0_w1, l0_w3, l0_w2)

    h = _rmsnorm(xt, l1_mixer_norm)
    cos, sin = _rope_tables(positions)
    qkv = _qkv_proj(h, l1_w_qkv, l1_q_norm, l1_k_norm, cos, sin, batch, seq)
    att = _moba_attention(qkv)
    xt = _matmul(att.reshape(batch * seq, d), l1_w_o, d, 0, "residual", F32, residual=xt)
    xt = _moe_layer(xt, l1_ffn_norm, l1_router_group, l1_router_expert, l1_w1, l1_w3, l1_w2)
    return xt.reshape(batch, seq, d)
```

```python
import functools

import jax
import jax.numpy as jnp
from jax import lax
from jax.experimental import pallas as pl
from jax.experimental.pallas import tpu as pltpu

NORM_EPS = 1e-6
NEG_INF = -1e30
ROPE_THETA = 10000.0
HEAD_DIM = 128
MOBA_BLOCK = 256
MOBA_TOPK = 3
TOPK_EXPERTS = 2
MOE_BLOCK = 256
SUB_ROWS = 256
LANES = 128
MIB = 1024 * 1024

BF16 = jnp.bfloat16
F32 = jnp.float32


def _params(vmem_mib, n_axes):
    return pltpu.CompilerParams(
        dimension_semantics=("arbitrary",) * n_axes,
        vmem_limit_bytes=vmem_mib * MIB)


def _rmsnorm_body(x_ref, g_ref, o_ref):
    x = x_ref[...]
    ms = jnp.mean(x * x, axis=-1, keepdims=True)
    o_ref[...] = (x * lax.rsqrt(ms + NORM_EPS) * g_ref[...]).astype(o_ref.dtype)


def _rmsnorm(x, g, tm=256):
    n, d = x.shape
    return pl.pallas_call(
        _rmsnorm_body,
        grid=(n // tm,),
        in_specs=[pl.BlockSpec((tm, d), lambda i: (i, 0)),
                  pl.BlockSpec((1, d), lambda i: (0, 0))],
        out_specs=pl.BlockSpec((tm, d), lambda i: (i, 0)),
        out_shape=jax.ShapeDtypeStruct((n, d), BF16),
        compiler_params=_params(32, 1),
        name="rmsnorm",
    )(x, g.reshape(1, d))


def _matmul_body(x_ref, w_ref, *rest, epilogue):
    if epilogue == "residual":
        r_ref, o_ref, wb_ref = rest
    else:
        o_ref, wb_ref = rest

    @pl.when(pl.program_id(1) == 0)
    def _():
        wb_ref[...] = w_ref[...].astype(BF16)

    sub = min(SUB_ROWS, x_ref.shape[0])
    for i in range(x_ref.shape[0] // sub):
        rows = pl.ds(i * sub, sub)
        acc = jnp.dot(x_ref[rows, :], wb_ref[...], preferred_element_type=F32)
        if epilogue == "gelu":
            acc = 0.5 * acc * (1.0 + lax.erf(acc * (2.0 ** -0.5)))
        elif epilogue == "residual":
            acc = r_ref[rows, :] + acc
        o_ref[rows, :] = acc.astype(o_ref.dtype)


def _matmul(x, w, n_out, col_offset, epilogue, out_dtype, residual=None, tm=1024, tn=512):
    m, k = x.shape
    tm = min(tm, m)
    col_block_offset = col_offset // tn
    in_specs = [pl.BlockSpec((tm, k), lambda j, i: (i, 0)),
                pl.BlockSpec((k, tn), lambda j, i: (0, j + col_block_offset))]
    args = [x, w]
    if epilogue == "residual":
        in_specs.append(pl.BlockSpec((tm, tn), lambda j, i: (i, j)))
        args.append(residual)
    return pl.pallas_call(
        functools.partial(_matmul_body, epilogue=epilogue),
        grid=(n_out // tn, m // tm),
        in_specs=in_specs,
        out_specs=pl.BlockSpec((tm, tn), lambda j, i: (i, j)),
        out_shape=jax.ShapeDtypeStruct((m, n_out), out_dtype),
        scratch_shapes=[pltpu.VMEM((k, tn), BF16)],
        compiler_params=_params(56, 2),
        name="matmul_" + epilogue,
    )(*args)


def _gate_body(u_ref, v_ref, lg_ref, lb_ref, ws_ref, bs_ref, o_ref, *, groups):
    v = v_ref[...]
    c, d = v.shape
    gd = d // groups
    mu = jnp.mean(v, axis=-1, keepdims=True)
    var = jnp.mean(jnp.square(v - mu), axis=-1, keepdims=True)
    vn = ((v - mu) * lax.rsqrt(var + NORM_EPS) * lg_ref[...] + lb_ref[...]).astype(BF16)
    row = lax.broadcasted_iota(jnp.int32, (c, c), 0)
    col = lax.broadcasted_iota(jnp.int32, (c, c), 1)
    causal = col <= row
    for g in range(groups):
        w = jnp.where(causal, ws_ref[g], 0.0).astype(BF16)
        s = jnp.dot(w, vn[:, g * gd:(g + 1) * gd], preferred_element_type=F32)
        s = s + bs_ref[:, g:g + 1]
        u = u_ref[:, g * gd:(g + 1) * gd].astype(F32)
        o_ref[:, g * gd:(g + 1) * gd] = (u * s).astype(o_ref.dtype)


def _spatial_gate(u, v, ln_g, ln_b, w_s, b_s):
    n, d = v.shape
    groups, c, _ = w_s.shape
    return pl.pallas_call(
        functools.partial(_gate_body, groups=groups),
        grid=(n // c,),
        in_specs=[pl.BlockSpec((c, d), lambda i: (i, 0)),
                  pl.BlockSpec((c, d), lambda i: (i, 0)),
                  pl.BlockSpec((1, d), lambda i: (0, 0)),
                  pl.BlockSpec((1, d), lambda i: (0, 0)),
                  pl.BlockSpec((groups, c, c), lambda i: (0, 0, 0)),
                  pl.BlockSpec((c, groups), lambda i: (0, 0))],
        out_specs=pl.BlockSpec((c, d), lambda i: (i, 0)),
        out_shape=jax.ShapeDtypeStruct((n, d), BF16),
        compiler_params=_params(32, 1),
        name="spatial_gate",
    )(u, v, ln_g.reshape(1, d), ln_b.reshape(1, d), w_s, b_s.T)


def _router_body(x_ref, g_ref, wr_ref, ids_ref, wts_ref, cnt_ref, xp_ref, run_ref, *, n_groups, per_group):
    @pl.when(pl.program_id(0) == 0)
    def _():
        run_ref[...] = jnp.zeros(run_ref.shape, run_ref.dtype)

    x = x_ref[...]
    tm = x.shape[0]
    ms = jnp.mean(x * x, axis=-1, keepdims=True)
    xn = (x * lax.rsqrt(ms + NORM_EPS) * g_ref[...]).astype(BF16)
    half = xn.shape[1] // 2
    lo = lax.bitcast_convert_type(xn[:, :half].astype(F32), jnp.uint32) >> 16
    hi = lax.bitcast_convert_type(xn[:, half:].astype(F32), jnp.uint32) & jnp.uint32(0xFFFF0000)
    xp_ref[...] = hi | lo
    logits = jnp.dot(xn, wr_ref[...], preferred_element_type=F32)
    lane = lax.broadcasted_iota(jnp.int32, logits.shape, 1)
    n_exp = n_groups * per_group
    big = jnp.int32(LANES)
    neg = jnp.float32(-jnp.inf)

    def first_argmax(vals, vmax):
        return jnp.min(jnp.where(vals == vmax, lane, big), axis=-1, keepdims=True)

    gl = jnp.where(lane < n_groups, logits, neg)
    gmax = jnp.max(gl, axis=-1, keepdims=True)
    ge = jnp.exp(gl - gmax)
    gprob = ge / jnp.sum(ge, axis=-1, keepdims=True)
    gate = jnp.max(gprob, axis=-1, keepdims=True)
    gidx = first_argmax(gprob, gate)
    e_lane = lane - n_groups
    in_group = (e_lane >= 0) & (e_lane < n_exp) & ((e_lane // per_group) == gidx)
    el = jnp.where(in_group, logits, neg)
    m1 = jnp.max(el, axis=-1, keepdims=True)
    i1 = first_argmax(el, m1)
    el2 = jnp.where(lane == i1, neg, el)
    m2 = jnp.max(el2, axis=-1, keepdims=True)
    i2 = first_argmax(el2, m2)
    e2 = jnp.exp(m2 - m1)
    denom = 1.0 + e2
    w1 = gate * (1.0 / denom)
    w2 = gate * (e2 / denom)
    e1 = i1 - n_groups
    e2 = i2 - n_groups
    onehot = jnp.where(lane == e1, 1.0, 0.0) + jnp.where(lane == e2, 1.0, 0.0)
    r_i = lax.broadcasted_iota(jnp.int32, (tm, tm), 0)
    c_i = lax.broadcasted_iota(jnp.int32, (tm, tm), 1)
    earlier = jnp.where(c_i < r_i, 1.0, 0.0).astype(BF16)
    before = jnp.dot(earlier, onehot.astype(BF16), preferred_element_type=F32) + run_ref[...]
    rank1 = jnp.sum(jnp.where(lane == e1, before, 0.0), axis=-1, keepdims=True).astype(jnp.int32)
    rank2 = jnp.sum(jnp.where(lane == e2, before, 0.0), axis=-1, keepdims=True).astype(jnp.int32)
    run_ref[...] = run_ref[...] + jnp.sum(onehot, axis=0, keepdims=True)
    cnt_ref[...] = run_ref[...].astype(jnp.int32)
    ids_ref[...] = jnp.where(lane == 0, e1, jnp.where(lane == 1, e2, jnp.where(
        lane == 2, rank1, jnp.where(lane == 3, rank2, 0))))
    wts_ref[...] = jnp.where(lane == 0, w1, jnp.where(lane == 1, w2, 0.0))


def _router(x, g, router_group, router_expert, tm=256):
    n, d = x.shape
    n_groups = router_group.shape[1]
    n_exp = router_expert.shape[1]
    wr = jnp.concatenate(
        [router_group, router_expert, jnp.zeros((d, LANES - n_groups - n_exp), F32)], axis=1)
    return pl.pallas_call(
        functools.partial(_router_body, n_groups=n_groups, per_group=n_exp // n_groups),
        grid=(n // tm,),
        in_specs=[pl.BlockSpec((tm, d), lambda i: (i, 0)),
                  pl.BlockSpec((1, d), lambda i: (0, 0)),
                  pl.BlockSpec((d, LANES), lambda i: (0, 0))],
        out_specs=[pl.BlockSpec((tm, LANES), lambda i: (i, 0)),
                   pl.BlockSpec((tm, LANES), lambda i: (i, 0)),
                   pl.BlockSpec((1, LANES), lambda i: (0, 0)),
                   pl.BlockSpec((tm, d // 2), lambda i: (i, 0))],
        out_shape=[jax.ShapeDtypeStruct((n, LANES), jnp.int32),
                   jax.ShapeDtypeStruct((n, LANES), F32),
                   jax.ShapeDtypeStruct((1, LANES), jnp.int32),
                   jax.ShapeDtypeStruct((n, d // 2), jnp.uint32)],
        scratch_shapes=[pltpu.VMEM((1, LANES), F32)],
        compiler_params=_params(32, 1),
        name="router",
    )(x, g.reshape(1, d), wr.astype(BF16))


def _row_copy(src_hbm, dst_vmem, src_row, dst_row, sem):
    return pltpu.make_async_copy(src_hbm.at[pl.ds(src_row, 1)], dst_vmem.at[pl.ds(dst_row, 1)], sem)


def _dispatch_body(valid_ref, tok_ref, xp_hbm, o_ref, buf, sem):
    rows, half = buf.shape

    @pl.when(valid_ref[pl.program_id(0)] == 1)
    def _():
        def issue(r, carry):
            _row_copy(xp_hbm, buf, tok_ref[0, 0, r], r, sem).start()
            return carry

        lax.fori_loop(0, rows, issue, 0, unroll=8)

        def drain(r, carry):
            _row_copy(xp_hbm, buf, 0, r, sem).wait()
            return carry

        lax.fori_loop(0, rows, drain, 0, unroll=8)
        w = buf[...]
        o_ref[:, :half] = lax.bitcast_convert_type(w << 16, F32).astype(o_ref.dtype)
        o_ref[:, half:] = lax.bitcast_convert_type(w & jnp.uint32(0xFFFF0000), F32).astype(o_ref.dtype)

    @pl.when(valid_ref[pl.program_id(0)] == 0)
    def _():
        o_ref[...] = jnp.zeros(o_ref.shape, o_ref.dtype)


def _dispatch(xp, row_tok, blk_valid):
    n, half = xp.shape
    n_blk = blk_valid.shape[0]
    return pl.pallas_call(
        _dispatch_body,
        grid_spec=pltpu.PrefetchScalarGridSpec(
            num_scalar_prefetch=1,
            grid=(n_blk,),
            in_specs=[pl.BlockSpec((1, 1, MOE_BLOCK), lambda b, bv: (b, 0, 0), memory_space=pltpu.SMEM),
                      pl.BlockSpec(memory_space=pl.ANY)],
            out_specs=pl.BlockSpec((MOE_BLOCK, 2 * half), lambda b, bv: (b, 0)),
            scratch_shapes=[pltpu.VMEM((MOE_BLOCK, half), jnp.uint32), pltpu.SemaphoreType.DMA(())]),
        out_shape=jax.ShapeDtypeStruct((n_blk * MOE_BLOCK, 2 * half), BF16),
        compiler_params=_params(32, 1),
        name="moe_dispatch",
    )(blk_valid, row_tok.reshape(n_blk, 1, MOE_BLOCK), xp)


def _up_body(se_ref, swj_ref, sj_ref, sb_ref, snew_ref, svalid_ref, sslot_ref, sne_ref, snj_ref,
             snext_ref, xs_ref, w1_hbm, w3_hbm, h_ref, stage, w1b_ref, w3b_ref, sem):
    s = pl.program_id(0)
    tf = w1b_ref.shape[1]
    slot = sslot_ref[s]

    def tile_copies(e, j, dst_slot):
        cols = pl.ds(pl.multiple_of(j * tf, tf), tf)
        return (pltpu.make_async_copy(w1_hbm.at[e, :, cols], stage.at[dst_slot, 0], sem.at[dst_slot, 0]),
                pltpu.make_async_copy(w3_hbm.at[e, :, cols], stage.at[dst_slot, 1], sem.at[dst_slot, 1]))

    def gated_unit(x, w1b, w3b):
        a = jnp.dot(x, w1b, preferred_element_type=F32)
        c = jnp.dot(x, w3b, preferred_element_type=F32)
        return (jax.nn.silu(a) * c).astype(h_ref.dtype)

    @pl.when(s == 0)
    def _():
        for cp in tile_copies(se_ref[0], swj_ref[0], slot):
            cp.start()

    @pl.when(snew_ref[s] == 1)
    def _():
        for cp in tile_copies(se_ref[s], swj_ref[s], slot):
            cp.wait()

        @pl.when(snext_ref[s] == 1)
        def _():
            for cp in tile_copies(sne_ref[s], snj_ref[s], 1 - slot):
                cp.start()

        w1b = stage[slot, 0].astype(BF16)
        w3b = stage[slot, 1].astype(BF16)
        w1b_ref[...] = w1b
        w3b_ref[...] = w3b
        h_ref[...] = gated_unit(xs_ref[...], w1b, w3b)

    @pl.when((snew_ref[s] == 0) & (svalid_ref[s] == 1))
    def _():
        h_ref[...] = gated_unit(xs_ref[...], w1b_ref[...], w3b_ref[...])

    @pl.when(svalid_ref[s] == 0)
    def _():
        h_ref[...] = jnp.zeros(h_ref.shape, h_ref.dtype)


def _expert_up(xs, w1, w3, sched, tf=512):
    n_rows, d = xs.shape
    f = w1.shape[2]
    n_steps = sched[0].shape[0]
    n_pref = len(sched)
    return pl.pallas_call(
        _up_body,
        grid_spec=pltpu.PrefetchScalarGridSpec(
            num_scalar_prefetch=n_pref,
            grid=(n_steps,),
            in_specs=[pl.BlockSpec((MOE_BLOCK, d), lambda s, *p: (p[3][s], 0)),
                      pl.BlockSpec(memory_space=pl.ANY),
                      pl.BlockSpec(memory_space=pl.ANY)],
            out_specs=pl.BlockSpec((MOE_BLOCK, tf), lambda s, *p: (p[3][s], p[2][s])),
            scratch_shapes=[pltpu.VMEM((2, 2, d, tf), F32),
                            pltpu.VMEM((d, tf), BF16), pltpu.VMEM((d, tf), BF16),
                            pltpu.SemaphoreType.DMA((2, 2))]),
        out_shape=jax.ShapeDtypeStruct((n_rows, f), BF16),
        compiler_params=_params(56, 1),
        name="expert_up",
    )(*sched, xs, w1, w3)


def _down_body(be_ref, bnew_ref, bvalid_ref, bslot_ref, bne_ref, bnext_ref, h_ref, w2_hbm, y_ref,
               stage, w2b_ref, sem):
    b = pl.program_id(0)
    slot = bslot_ref[b]

    def weight_copy(e, dst_slot):
        return pltpu.make_async_copy(w2_hbm.at[e], stage.at[dst_slot], sem.at[dst_slot])

    @pl.when(b == 0)
    def _():
        weight_copy(be_ref[0], slot).start()

    @pl.when(bnew_ref[b] == 1)
    def _():
        weight_copy(be_ref[b], slot).wait()

        @pl.when(bnext_ref[b] == 1)
        def _():
            weight_copy(bne_ref[b], 1 - slot).start()

        w2b = stage[slot].astype(BF16)
        w2b_ref[...] = w2b
        y_ref[...] = jnp.dot(h_ref[...], w2b, preferred_element_type=F32)

    @pl.when((bnew_ref[b] == 0) & (bvalid_ref[b] == 1))
    def _():
        y_ref[...] = jnp.dot(h_ref[...], w2b_ref[...], preferred_element_type=F32)

    @pl.when(bvalid_ref[b] == 0)
    def _():
        y_ref[...] = jnp.zeros(y_ref.shape, y_ref.dtype)


def _expert_down(h, w2, down_sched):
    n_rows, f = h.shape
    d = w2.shape[2]
    n_blk = n_rows // MOE_BLOCK
    return pl.pallas_call(
        _down_body,
        grid_spec=pltpu.PrefetchScalarGridSpec(
            num_scalar_prefetch=len(down_sched),
            grid=(n_blk,),
            in_specs=[pl.BlockSpec((MOE_BLOCK, f), lambda b, *p: (b, 0)),
                      pl.BlockSpec(memory_space=pl.ANY)],
            out_specs=pl.BlockSpec((MOE_BLOCK, d), lambda b, *p: (b, 0)),
            scratch_shapes=[pltpu.VMEM((2, f, d), F32), pltpu.VMEM((f, d), BF16),
                            pltpu.SemaphoreType.DMA((2,))]),
        out_shape=jax.ShapeDtypeStruct((n_rows, d), F32),
        compiler_params=_params(56, 1),
        name="expert_down",
    )(*down_sched, h, w2)


def _combine_body(slot_ref, x_ref, w_ref, y_hbm, *rest, fuse_norm):
    if fuse_norm:
        g_ref, o_ref, h_ref, buf, sem = rest
    else:
        o_ref, buf, sem = rest
    tm = x_ref.shape[0]

    def issue(r, carry):
        for kk in range(TOPK_EXPERTS):
            _row_copy(y_hbm, buf.at[kk], slot_ref[0, kk, r], r, sem).start()
        return carry

    lax.fori_loop(0, tm, issue, 0, unroll=4)

    def drain(r, carry):
        for kk in range(TOPK_EXPERTS):
            _row_copy(y_hbm, buf.at[kk], 0, r, sem).wait()
        return carry

    lax.fori_loop(0, tm, drain, 0, unroll=4)
    acc = buf[0] * w_ref[:, 0:1]
    for kk in range(1, TOPK_EXPERTS):
        acc = acc + buf[kk] * w_ref[:, kk:kk + 1]
    out = x_ref[...] + acc
    o_ref[...] = out
    if fuse_norm:
        ms = jnp.mean(out * out, axis=-1, keepdims=True)
        h_ref[...] = (out * lax.rsqrt(ms + NORM_EPS) * g_ref[...]).astype(h_ref.dtype)


def _combine(x, y, slots, wts, next_norm_g=None, tm=256):
    n, d = x.shape
    fuse_norm = next_norm_g is not None
    slots_t = slots.reshape(n // tm, tm, TOPK_EXPERTS).transpose(0, 2, 1)
    row_spec = pl.BlockSpec((tm, d), lambda i: (i, 0))
    in_specs = [pl.BlockSpec((1, TOPK_EXPERTS, tm), lambda i: (i, 0, 0), memory_space=pltpu.SMEM),
                row_spec,
                pl.BlockSpec((tm, LANES), lambda i: (i, 0)),
                pl.BlockSpec(memory_space=pl.ANY)]
    args = [slots_t, x, wts, y]
    out_specs = row_spec
    out_shape = jax.ShapeDtypeStruct((n, d), F32)
    if fuse_norm:
        in_specs.append(pl.BlockSpec((1, d), lambda i: (0, 0)))
        args.append(next_norm_g.reshape(1, d))
        out_specs = [row_spec, row_spec]
        out_shape = [out_shape, jax.ShapeDtypeStruct((n, d), BF16)]
    return pl.pallas_call(
        functools.partial(_combine_body, fuse_norm=fuse_norm),
        grid=(n // tm,),
        in_specs=in_specs,
        out_specs=out_specs,
        out_shape=out_shape,
        scratch_shapes=[pltpu.VMEM((TOPK_EXPERTS, tm, d), F32), pltpu.SemaphoreType.DMA(())],
        compiler_params=_params(40, 1),
        name="moe_combine",
    )(*args)


def _moe_plan(ids, counts, n_exp, tiles_per_expert):
    n_tok = ids.shape[0]
    n_assign = n_tok * TOPK_EXPERTS
    eid = ids[:, :TOPK_EXPERTS].reshape(-1)
    rank = ids[:, TOPK_EXPERTS:2 * TOPK_EXPERTS].reshape(-1)
    counts = counts[0, :n_exp]
    padded = (counts + MOE_BLOCK - 1) // MOE_BLOCK * MOE_BLOCK
    pad_end = jnp.cumsum(padded)
    pad_start = pad_end - padded
    slot = pad_start[eid] + rank
    n_rows = n_assign + n_exp * MOE_BLOCK
    n_blk = n_rows // MOE_BLOCK
    token_id = jnp.arange(n_assign, dtype=jnp.int32) // TOPK_EXPERTS
    row_tok = (jnp.arange(n_rows, dtype=jnp.int32) % n_tok).at[slot].set(token_id)
    blk_ids = jnp.arange(n_blk, dtype=jnp.int32)
    blk_e = jnp.minimum(
        jnp.searchsorted(pad_end, blk_ids * MOE_BLOCK, side='right'), n_exp - 1).astype(jnp.int32)
    n_used = (pad_end[-1] // MOE_BLOCK).astype(jnp.int32)
    last = jnp.maximum(n_used - 1, 0)
    blk_valid = (blk_ids < n_used).astype(jnp.int32)
    blk_e_eff = jnp.where(blk_ids < n_used, blk_e, blk_e[last])
    blk_new = jnp.concatenate(
        [jnp.ones((1,), jnp.int32), (blk_e_eff[1:] != blk_e_eff[:-1]).astype(jnp.int32)])
    blk_slot, blk_next_at, blk_has_next = _group_lookahead(blk_new)
    down_sched = (blk_e_eff, blk_new, blk_valid, blk_slot, blk_e_eff[blk_next_at], blk_has_next)
    blocks_of = (padded // MOE_BLOCK).astype(jnp.int32)
    first_blk = (pad_start // MOE_BLOCK).astype(jnp.int32)
    steps = jnp.arange(n_blk * tiles_per_expert, dtype=jnp.int32)
    n_valid = n_used * tiles_per_expert
    valid = steps < n_valid
    s_eff = jnp.where(valid, steps, jnp.maximum(n_valid - 1, 0))
    s_e = blk_e[s_eff // tiles_per_expert]
    local = s_eff - tiles_per_expert * first_blk[s_e]
    nb = jnp.maximum(blocks_of[s_e], 1)
    s_wj = (local // nb).astype(jnp.int32)
    extra = steps - n_valid
    s_j = jnp.where(valid, s_wj, extra % tiles_per_expert).astype(jnp.int32)
    s_b = jnp.where(valid, first_blk[s_e] + local % nb, n_used + extra // tiles_per_expert).astype(jnp.int32)
    key = s_e * tiles_per_expert + s_wj
    s_new = jnp.concatenate([jnp.ones((1,), jnp.int32), (key[1:] != key[:-1]).astype(jnp.int32)])
    s_e = s_e.astype(jnp.int32)
    s_slot, s_next_at, s_has_next = _group_lookahead(s_new)
    sched = (s_e, s_wj, s_j, s_b, s_new, valid.astype(jnp.int32),
             s_slot, s_e[s_next_at], s_wj[s_next_at], s_has_next)
    return slot.reshape(n_tok, TOPK_EXPERTS), row_tok, sched, down_sched


def _group_lookahead(new_flags):
    n = new_flags.shape[0]
    idx = jnp.arange(n, dtype=jnp.int32)
    slot = ((jnp.cumsum(new_flags) - 1) % 2).astype(jnp.int32)
    starts_from = lax.cummin(jnp.where(new_flags == 1, idx, n), reverse=True)
    next_at = jnp.concatenate([starts_from[1:], jnp.full((1,), n, jnp.int32)])
    has_next = (next_at < n).astype(jnp.int32)
    return slot, jnp.minimum(next_at, n - 1), has_next


def _moe_layer(x, norm_g, router_group, router_expert, w1, w3, w2, next_norm_g=None):
    n_exp, _, f = w1.shape
    tf = min(512, f)
    ids, wts, counts, xp = _router(x, norm_g, router_group, router_expert)
    slots, row_tok, sched, down_sched = _moe_plan(ids, counts, n_exp, f // tf)
    xs = _dispatch(xp, row_tok, down_sched[2])
    h = _expert_up(xs, w1, w3, sched, tf=tf)
    y = _expert_down(h, w2, down_sched)
    return _combine(x, y, slots, wts, next_norm_g)


def _rope_body(pos_ref, inv_ref, cos_ref, sin_ref):
    ang = pos_ref[...].astype(F32) * inv_ref[...]
    cos_ref[...] = jnp.cos(ang)
    sin_ref[...] = jnp.sin(ang)


def _rope_tables(positions, tm=1024):
    n = positions.size
    half = ROPE_THETA ** (-jnp.arange(0, HEAD_DIM, 2, dtype=F32) / HEAD_DIM)
    inv = jnp.concatenate([half, half]).reshape(1, HEAD_DIM)
    return pl.pallas_call(
        _rope_body,
        grid=(n // tm,),
        in_specs=[pl.BlockSpec((tm, 1), lambda i: (i, 0)),
                  pl.BlockSpec((1, HEAD_DIM), lambda i: (0, 0))],
        out_specs=[pl.BlockSpec((tm, HEAD_DIM), lambda i: (i, 0)),
                   pl.BlockSpec((tm, HEAD_DIM), lambda i: (i, 0))],
        out_shape=[jax.ShapeDtypeStruct((n, HEAD_DIM), F32),
                   jax.ShapeDtypeStruct((n, HEAD_DIM), F32)],
        compiler_params=_params(32, 1),
        name="rope_tables",
    )(positions.reshape(n, 1), inv)


def _qkv_body(x_ref, w_ref, g_ref, cos_ref, sin_ref, o_ref, wb_ref, *, tiles_per_part, heads_per_tile):
    j = pl.program_id(0)

    @pl.when(pl.program_id(1) == 0)
    def _():
        wb_ref[...] = w_ref[...].astype(BF16)

    sub = min(SUB_ROWS, x_ref.shape[0])
    n_sub = x_ref.shape[0] // sub

    def sub_dot(i):
        return jnp.dot(x_ref[pl.ds(i * sub, sub), :], wb_ref[...], preferred_element_type=F32)

    @pl.when(j < 2 * tiles_per_part)
    def _():
        gain = g_ref[0]
        lane = lax.broadcasted_iota(jnp.int32, (sub, HEAD_DIM), 1)
        for i in range(n_sub):
            rows = pl.ds(i * sub, sub)
            acc = sub_dot(i)
            cos = cos_ref[rows, :]
            sin = sin_ref[rows, :]
            for hh in range(heads_per_tile):
                a = acc[:, hh * HEAD_DIM:(hh + 1) * HEAD_DIM]
                ms = jnp.mean(a * a, axis=-1, keepdims=True)
                y = a * lax.rsqrt(ms + NORM_EPS) * gain
                rolled = pltpu.roll(y, HEAD_DIM // 2, axis=1)
                rot = jnp.where(lane < HEAD_DIM // 2, -rolled, rolled)
                o_ref[0, 0, hh, rows, :] = (y * cos + rot * sin).astype(o_ref.dtype)

    @pl.when(j >= 2 * tiles_per_part)
    def _():
        for i in range(n_sub):
            acc = sub_dot(i)
            for hh in range(heads_per_tile):
                o_ref[0, 0, hh, pl.ds(i * sub, sub), :] = (
                    acc[:, hh * HEAD_DIM:(hh + 1) * HEAD_DIM].astype(o_ref.dtype))


def _qkv_proj(x, w_qkv, q_gain, k_gain, cos, sin, batch, seq, tm=1024, tn=512):
    m, k = x.shape
    tm = min(tm, seq)
    d_att = w_qkv.shape[1] // 3
    n_heads = d_att // HEAD_DIM
    tiles_per_part = d_att // tn
    heads_per_tile = tn // HEAD_DIM
    m_per_batch = seq // tm
    gains = jnp.stack([q_gain, k_gain, jnp.ones_like(q_gain)]).reshape(3, 1, HEAD_DIM)
    return pl.pallas_call(
        functools.partial(_qkv_body, tiles_per_part=tiles_per_part, heads_per_tile=heads_per_tile),
        grid=(3 * tiles_per_part, m // tm),
        in_specs=[pl.BlockSpec((tm, k), lambda j, i: (i, 0)),
                  pl.BlockSpec((k, tn), lambda j, i: (0, j)),
                  pl.BlockSpec((1, 1, HEAD_DIM), lambda j, i: (j // tiles_per_part, 0, 0)),
                  pl.BlockSpec((tm, HEAD_DIM), lambda j, i: (i, 0)),
                  pl.BlockSpec((tm, HEAD_DIM), lambda j, i: (i, 0))],
        out_specs=pl.BlockSpec(
            (1, 1, heads_per_tile, tm, HEAD_DIM),
            lambda j, i: (j // tiles_per_part, i // m_per_batch, j % tiles_per_part,
                          i % m_per_batch, 0)),
        out_shape=jax.ShapeDtypeStruct((3, batch, n_heads, seq, HEAD_DIM), BF16),
        scratch_shapes=[pltpu.VMEM((k, tn), BF16)],
        compiler_params=_params(56, 2),
        name="qkv_proj",
    )(x, w_qkv, gains, cos, sin)


def _moba_body(q_ref, k_ref, v_ref, o_ref):
    q = q_ref[0, 0, 0]
    k = k_ref[0, 0, 0]
    v = v_ref[0, 0, 0]
    seq = q.shape[0]
    blk = MOBA_BLOCK
    n_blocks = seq // blk
    exp2_scale = (HEAD_DIM ** -0.5) * 1.4426950408889634
    nt_dims = (((1,), (1,)), ((), ()))

    bi = lax.broadcasted_iota(jnp.int32, (LANES, seq), 0)
    ki = lax.broadcasted_iota(jnp.int32, (LANES, seq), 1)
    indicator = jnp.where(ki // blk == bi, 1.0, 0.0).astype(BF16)
    k_mean = jnp.dot(indicator, k, preferred_element_type=F32) * (1.0 / blk)
    gate_all = lax.dot_general(q, k_mean.astype(BF16), nt_dims, preferred_element_type=F32)

    kr = lax.broadcasted_iota(jnp.int32, (seq, LANES), 0)
    kl = lax.broadcasted_iota(jnp.int32, (seq, LANES), 1)
    k_ext = jnp.concatenate([k, jnp.where(kr // blk == kl, NEG_INF, 0.0).astype(BF16)], axis=1)

    row = lax.broadcasted_iota(jnp.int32, (blk, blk), 0)
    col = lax.broadcasted_iota(jnp.int32, (blk, blk), 1)
    causal = col <= row
    blane = lax.broadcasted_iota(jnp.int32, (blk, LANES), 1)

    for j in range(n_blocks):
        qj = q[j * blk:(j + 1) * blk]
        s_own = lax.dot_general(qj, k[j * blk:(j + 1) * blk], nt_dims, preferred_element_type=F32)
        s_own = jnp.where(causal, s_own, NEG_INF)
        m = jnp.max(s_own, axis=-1, keepdims=True)
        n_past = j * blk
        if j > 0:
            g = jnp.where(blane < j, gate_all[j * blk:(j + 1) * blk], NEG_INF)
            unselected = jnp.ones((blk, LANES), F32)
            for _ in range(min(MOBA_TOPK, j)):
                gm = jnp.max(g, axis=-1, keepdims=True)
                pick = blane == jnp.min(jnp.where(g == gm, blane, LANES), axis=-1, keepdims=True)
                unselected = jnp.where(pick, 0.0, unselected)
                g = jnp.where(pick, -jnp.inf, g)
            q_ext = jnp.concatenate([qj, unselected.astype(BF16)], axis=1)
            s_past = lax.dot_general(q_ext, k_ext[:n_past], nt_dims, preferred_element_type=F32)
            m = jnp.maximum(m, jnp.max(s_past, axis=-1, keepdims=True))
            e_past = jnp.exp2((s_past - m) * exp2_scale)
        e_own = jnp.exp2((s_own - m) * exp2_scale)
        total = jnp.sum(e_own, axis=-1, keepdims=True)
        o = jnp.dot(e_own.astype(BF16), v[j * blk:(j + 1) * blk], preferred_element_type=F32)
        if j > 0:
            total = total + jnp.sum(e_past, axis=-1, keepdims=True)
            o = o + jnp.dot(e_past.astype(BF16), v[:n_past], preferred_element_type=F32)
        o_ref[0, j * blk:(j + 1) * blk, :] = (o * (1.0 / total)).astype(o_ref.dtype)


def _moba_attention(qkv):
    _, batch, n_heads, seq, hd = qkv.shape

    def part(p):
        return pl.BlockSpec((1, 1, 1, seq, hd), lambda b, h: (p, b, h, 0, 0))

    return pl.pallas_call(
        _moba_body,
        grid=(batch, n_heads),
        in_specs=[part(0), part(1), part(2)],
        out_specs=pl.BlockSpec((1, seq, hd), lambda b, h: (b, 0, h)),
        out_shape=jax.ShapeDtypeStruct((batch, seq, n_heads * hd), BF16),
        compiler_params=_params(48, 2),
        name="moba_attention",
    )(qkv, qkv, qkv)


def kernel(x, positions, l0_mixer_norm, l0_gmlp_w_in, l0_gmlp_ln_g, l0_gmlp_ln_b, l0_gmlp_w_s, l0_gmlp_b_s, l0_gmlp_w_out, l0_ffn_norm, l0_router_group, l0_router_expert, l0_w1, l0_w3, l0_w2, l1_mixer_norm, l1_w_qkv, l1_q_norm, l1_k_norm, l1_w_o, l1_ffn_norm, l1_router_group, l1_router_expert, l1_w1, l1_w3, l1_w2):
    batch, seq, d = x.shape
    xt = x.reshape(batch * seq, d)

    h = _rmsnorm(xt, l0_mixer_norm)
    d_mix = l0_gmlp_w_in.shape[1] // 2
    u = _matmul(h, l0_gmlp_w_in, d_mix, 0, "gelu", BF16)
    v = _matmul(h, l0_gmlp_w_in, d_mix, d_mix, "gelu", F32)
    gated = _spatial_gate(u, v, l0_gmlp_ln_g, l0_gmlp_ln_b, l0_gmlp_w_s, l0_gmlp_b_s)
    xt = _matmul(gated, l0_gmlp_w_out, d, 0, "residual", F32, residual=xt)
    xt, h = _moe_layer(xt, l0_ffn_norm, l0_router_group, l0_router_expert, l0_w1, l0_w3, l0_w2,
                       next_norm_g=l1_mixer_norm)

    cos, sin = _rope_tables(positions)
    qkv = _qkv_proj(h, l1_w_qkv, l1_q_norm, l1_k_norm, cos, sin, batch, seq)
    att = _moba_attention(qkv)
    xt = _matmul(att.reshape(batch * seq, d), l1_w_o, d, 0, "residual", F32, residual=xt)
    xt = _moe_layer(xt, l1_ffn_norm, l1_router_group, l1_router_expert, l1_w1, l1_w3, l1_w2)
    return xt.reshape(batch, seq, d)
```
